```python
import math
import jax
import jax.numpy as jnp
from jax import lax
import numpy as np

D_MODEL = 1024
BATCH = 2
SEQ = 8192
DEPTH = 2
DEC_BATCH = 128
DEC_SEQ = 8
PAST_LEN = 2048
PAGE_SIZE = 128

N_A = DEPTH // 2
N_B = DEPTH - N_A
N_DENSE = (DEPTH + 1) // 2
N_MOE = DEPTH // 2

ML_INNER = 2 * D_MODEL
ML_HEADS = 4
ML_HD = ML_INNER // ML_HEADS
ML_CONV = 4
ML_CHUNK = 64

DA_HEADS = 8
DA_DK = D_MODEL // (2 * DA_HEADS)
DA_DV = 2 * DA_DK
DA_QW = DA_HEADS * 2 * DA_DK
DA_VW = DA_HEADS * DA_DV
Q_BLOCK = 128

D_FF = 2816
N_EXPERTS = 8
TOP_K = 2
D_FF_EXPERT = 2816

EPS = 1e-6

kernel_name = "yoco_mlstm_diffattn_decoder_step"


def rmsnorm(x, g):
    xf = x.astype(jnp.float32)
    y = xf * lax.rsqrt(jnp.mean(xf * xf, axis=-1, keepdims=True) + EPS)
    return (y * g.astype(jnp.float32)).astype(x.dtype)


def causal_conv(x, buf, w, b):
    T = x.shape[1]
    xp = jnp.concatenate([buf, x], axis=1)
    y = b
    for i in range(ML_CONV):
        y = y + xp[:, i:i + T] * w[i]
    return y, xp[:, xp.shape[1] - (ML_CONV - 1):]


def mlstm_scan(q, k, v, ig, lf, C0, n0, m0, chunk):
    B, T, H, _ = q.shape
    nc = T // chunk

    def to_chunks(a):
        return jnp.moveaxis(a.reshape((B, nc, chunk) + a.shape[2:]), 1, 0)

    causal = jnp.tril(jnp.ones((chunk, chunk), dtype=bool))

    def step(carry, xs):
        C, n, m = carry
        qc, kc, vc, igc, lfc = xs
        b = jnp.cumsum(lfc, axis=1).transpose(0, 2, 1)
        igh = igc.transpose(0, 2, 1)
        d_log = b[..., :, None] - b[..., None, :] + igh[..., None, :]
        d_log = jnp.where(causal, d_log, -jnp.inf)
        inter = b + m[..., None]
        m_t = jnp.maximum(inter, jnp.max(d_log, axis=-1))
        dw = jnp.exp(d_log - m_t[..., None])
        w_inter = jnp.exp(inter - m_t)
        s = jnp.einsum('blhd,bshd->bhls', qc, kc) * dw
        num = w_inter[..., None] * jnp.einsum('blhd,bhdv->bhlv', qc, C) + jnp.einsum('bhls,bshv->bhlv', s, vc)
        den = w_inter * jnp.einsum('blhd,bhd->bhl', qc, n) + jnp.sum(s, axis=-1)
        h = num / jnp.maximum(jnp.abs(den), jnp.exp(-m_t))[..., None]
        w_last = dw[..., -1, :].transpose(0, 2, 1)
        scale0 = w_inter[..., -1]
        kw = kc * w_last[..., None]
        C_new = scale0[..., None, None] * C + jnp.einsum('bshd,bshv->bhdv', kw, vc)
        n_new = scale0[..., None] * n + jnp.sum(kw, axis=1)
        return (C_new, n_new, m_t[..., -1]), h.transpose(0, 2, 1, 3)

    (C, n, m), hs = lax.scan(step, (C0, n0, m0), (to_chunks(q), to_chunks(k), to_chunks(v), to_chunks(ig), to_chunks(lf)))
    h = jnp.moveaxis(hs, 0, 1).reshape(B, T, H, v.shape[-1])
    return h, C, n, m


def mlstm_layer(x, C0, n0, m0, conv0, chunk, g_norm, w_up, conv_w, conv_b, w_q, w_k, w_v,
                w_ig, b_ig, w_fg, b_fg, g_head, skip, w_down):
    B, T, _ = x.shape
    f32 = jnp.float32
    up = rmsnorm(x, g_norm) @ w_up
    xm, z = up[..., :ML_INNER], up[..., ML_INNER:]
    xc, conv_new = causal_conv(xm, conv0.astype(xm.dtype), conv_w, conv_b)
    xc = jax.nn.silu(xc)
    q = xc @ w_q
    k = xc @ w_k
    v = xm @ w_v
    g_in = jnp.concatenate([q, k, v], axis=-1)
    ig = (g_in @ w_ig + b_ig).astype(f32)
    lf = jax.nn.log_sigmoid((g_in @ w_fg + b_fg).astype(f32))

    def heads(a):
        return a.reshape(B, T, ML_HEADS, ML_HD).astype(f32)

    h, C, n, m = mlstm_scan(heads(q), heads(k) * (ML_HD ** -0.5), heads(v), ig, lf,
                            C0.astype(f32), n0.astype(f32), m0.astype(f32), chunk)
    h = rmsnorm(h, g_head.reshape(ML_HEADS, ML_HD)).reshape(B, T, ML_INNER).astype(x.dtype)
    out = ((h + skip * xc) * jax.nn.silu(z)) @ w_down
    return out, C, n, m, conv_new


def shared_kv(x, g, w_kv):
    B, T, _ = x.shape
    kv = rmsnorm(x, g) @ w_kv
    k = kv[..., :DA_QW].reshape(B, T, DA_HEADS, 2, DA_DK)
    v = kv[..., DA_QW:].reshape(B, T, DA_HEADS, DA_DV)
    return k, v


def alibi_slopes():
    return jnp.exp2(-8.0 * jnp.arange(1, DA_HEADS + 1, dtype=jnp.float32) / DA_HEADS)


def diff_attn_block(q, k, v, q_pos, k_pos, lam, slopes):
    s = jnp.einsum('bqhcd,bkhcd->bhcqk', q, k).astype(jnp.float32) * (DA_DK ** -0.5)
    dist = (q_pos[:, None] - k_pos[None, :]).astype(jnp.float32)
    s = s + (-slopes[:, None, None] * dist)[None, :, None]
    s = jnp.where(dist >= 0, s, -jnp.inf)
    p = jax.nn.softmax(s, axis=-1)
    a = p[:, :, 0] - lam * p[:, :, 1]
    return jnp.einsum('bhqk,bkhv->bqhv', a.astype(v.dtype), v)


def diff_attn_layer(x, k, v, q_pos, k_pos, g_norm, w_q, lq1, lk1, lq2, lk2, g_sub, w_o, lam_init):
    B, T, _ = x.shape
    q = (rmsnorm(x, g_norm) @ w_q).reshape(B, T, DA_HEADS, 2, DA_DK)
    lam = (jnp.exp(jnp.sum(lq1 * lk1).astype(jnp.float32))
           - jnp.exp(jnp.sum(lq2 * lk2).astype(jnp.float32)) + lam_init)
    slopes = alibi_slopes()
    if T > Q_BLOCK and T % Q_BLOCK == 0:
        nb = T // Q_BLOCK
        qb = jnp.moveaxis(q.reshape(B, nb, Q_BLOCK, DA_HEADS, 2, DA_DK), 1, 0)
        pb = q_pos.reshape(nb, Q_BLOCK)
        o = lax.map(lambda a: diff_attn_block(a[0], k, v, a[1], k_pos, lam, slopes), (qb, pb))
        o = jnp.moveaxis(o, 0, 1).reshape(B, T, DA_HEADS, DA_DV)
    else:
        o = diff_attn_block(q, k, v, q_pos, k_pos, lam, slopes)
    o = rmsnorm(o, g_sub) * (1.0 - lam_init)
    return o.reshape(B, T, DA_VW) @ w_o


def swiglu(x, w_gate, w_up, w_down):
    return (jax.nn.silu(x @ w_gate) * (x @ w_up)) @ w_down


def moe(x, w_router, w_gate, w_up, w_down):
    logits = (x @ w_router).astype(jnp.float32)
    top_v, top_i = lax.top_k(logits, TOP_K)
    gates = jax.nn.softmax(top_v, axis=-1)
    dense_gate = jnp.sum(jax.nn.one_hot(top_i, N_EXPERTS, dtype=jnp.float32) * gates[..., None], axis=-2)
    dense_gate = dense_gate.astype(x.dtype)
    y = jnp.zeros_like(x)
    for e in range(N_EXPERTS):
        y = y + dense_gate[..., e:e + 1] * swiglu(x, w_gate[e], w_up[e], w_down[e])
    return y


def setup_inputs(seed: int = 0) -> dict:
    key = jax.random.key(seed)
    ks = iter(jax.random.split(key, 64))
    f32 = jnp.float32

    def nrm(shape, scale):
        return jax.random.normal(next(ks), shape, f32) * scale

    def gain(shape):
        return 1.0 + nrm(shape, 0.05)

    n_pages = PAST_LEN // PAGE_SIZE
    n_used = DEC_BATCH * n_pages
    n_pool = n_used + (n_used + 3) // 4
    inp = {}
    inp['x_prompt'] = nrm((BATCH, SEQ, D_MODEL), 1.0)
    inp['x_sample'] = nrm((DEC_BATCH, DEC_SEQ, D_MODEL), 1.0)
    inp['state_mlstm_C'] = nrm((N_A, DEC_BATCH, ML_HEADS, ML_HD, ML_HD), 0.05)
    inp['state_mlstm_n'] = nrm((N_A, DEC_BATCH, ML_HEADS, ML_HD), 0.05)
    inp['state_mlstm_m'] = nrm((N_A, DEC_BATCH, ML_HEADS), 1.0)
    inp['state_conv'] = nrm((N_A, DEC_BATCH, ML_CONV - 1, ML_INNER), 1.0)
    inp['cache_k'] = nrm((n_pool, PAGE_SIZE, DA_HEADS, 2, DA_DK), 1.0)
    inp['cache_v'] = nrm((n_pool, PAGE_SIZE, DA_HEADS, DA_DV), 1.0)
    perm = jax.random.permutation(next(ks), n_pool)
    inp['page_table'] = perm[:n_used].reshape(DEC_BATCH, n_pages).astype(jnp.int32)
    inp['ml_norm'] = gain((N_A, D_MODEL))
    inp['ml_w_up'] = nrm((N_A, D_MODEL, 2 * ML_INNER), D_MODEL ** -0.5)
    inp['ml_conv_w'] = nrm((N_A, ML_CONV, ML_INNER), 0.5)
    inp['ml_conv_b'] = nrm((N_A, ML_INNER), 0.02)
    inp['ml_w_q'] = nrm((N_A, ML_INNER, ML_INNER), ML_INNER ** -0.5)
    inp['ml_w_k'] = nrm((N_A, ML_INNER, ML_INNER), ML_INNER ** -0.5)
    inp['ml_w_v'] = nrm((N_A, ML_INNER, ML_INNER), ML_INNER ** -0.5)
    inp['ml_w_ig'] = nrm((N_A, 3 * ML_INNER, ML_HEADS), 0.1 * (3 * ML_INNER) ** -0.5)
    inp['ml_b_ig'] = nrm((N_A, ML_HEADS), 0.1)
    inp['ml_w_fg'] = nrm((N_A, 3 * ML_INNER, ML_HEADS), 0.1 * (3 * ML_INNER) ** -0.5)
    inp['ml_b_fg'] = jnp.linspace(3.0, 6.0, ML_HEADS, dtype=f32)[None, :] + nrm((N_A, ML_HEADS), 0.1)
    inp['ml_head_norm'] = gain((N_A, ML_INNER))
    inp['ml_skip'] = gain((N_A, ML_INNER))
    inp['ml_w_down'] = nrm((N_A, ML_INNER, D_MODEL), ML_INNER ** -0.5)
    inp['kv_norm'] = gain((D_MODEL,))
    inp['w_kv'] = nrm((D_MODEL, DA_QW + DA_VW), D_MODEL ** -0.5)
    inp['da_norm'] = gain((N_B, D_MODEL))
    inp['da_w_q'] = nrm((N_B, D_MODEL, DA_QW), D_MODEL ** -0.5)
    inp['da_lq1'] = nrm((N_B, DA_DK), 0.1)
    inp['da_lk1'] = nrm((N_B, DA_DK), 0.1)
    inp['da_lq2'] = nrm((N_B, DA_DK), 0.1)
    inp['da_lk2'] = nrm((N_B, DA_DK), 0.1)
    inp['da_subln'] = gain((N_B, DA_DV))
    inp['da_w_o'] = nrm((N_B, DA_VW, D_MODEL), DA_VW ** -0.5)
    inp['ffn_norm'] = gain((DEPTH, D_MODEL))
    inp['mlp_w_gate'] = nrm((N_DENSE, D_MODEL, D_FF), D_MODEL ** -0.5)
    inp['mlp_w_up'] = nrm((N_DENSE, D_MODEL, D_FF), D_MODEL ** -0.5)
    inp['mlp_w_down'] = nrm((N_DENSE, D_FF, D_MODEL), D_FF ** -0.5)
    inp['moe_router'] = nrm((N_MOE, D_MODEL, N_EXPERTS), D_MODEL ** -0.5)
    inp['moe_w_gate'] = nrm((N_MOE, N_EXPERTS, D_MODEL, D_FF_EXPERT), D_MODEL ** -0.5)
    inp['moe_w_up'] = nrm((N_MOE, N_EXPERTS, D_MODEL, D_FF_EXPERT), D_MODEL ** -0.5)
    inp['moe_w_down'] = nrm((N_MOE, N_EXPERTS, D_FF_EXPERT, D_MODEL), D_FF_EXPERT ** -0.5)
    inp['final_norm'] = gain((D_MODEL,))
    return inp


def reference(x_prompt, x_sample, state_mlstm_C, state_mlstm_n, state_mlstm_m, state_conv,
              cache_k, cache_v, page_table,
              ml_norm, ml_w_up, ml_conv_w, ml_conv_b, ml_w_q, ml_w_k, ml_w_v,
              ml_w_ig, ml_b_ig, ml_w_fg, ml_b_fg, ml_head_norm, ml_skip, ml_w_down,
              kv_norm, w_kv, da_norm, da_w_q, da_lq1, da_lk1, da_lq2, da_lk2, da_subln, da_w_o,
              ffn_norm, mlp_w_gate, mlp_w_up, mlp_w_down,
              moe_router, moe_w_gate, moe_w_up, moe_w_down, final_norm):
    f32 = jnp.float32

    def run(x, C0, n0, m0, conv0, k_past, v_past, chunk):
        T = x.shape[1]
        P = k_past.shape[1]
        q_pos = P + jnp.arange(T, dtype=jnp.int32)
        k_pos = jnp.arange(P + T, dtype=jnp.int32)
        Cs, ns, ms, convs = [], [], [], []
        k_new = v_new = k_all = v_all = None
        for l in range(DEPTH):
            if l < N_A:
                h, C, n, m, cb = mlstm_layer(
                    x, C0[l], n0[l], m0[l], conv0[l], chunk, ml_norm[l], ml_w_up[l], ml_conv_w[l],
                    ml_conv_b[l], ml_w_q[l], ml_w_k[l], ml_w_v[l], ml_w_ig[l], ml_b_ig[l],
                    ml_w_fg[l], ml_b_fg[l], ml_head_norm[l], ml_skip[l], ml_w_down[l])
                x = x + h
                Cs.append(C)
                ns.append(n)
                ms.append(m)
                convs.append(cb)
            else:
                if l == N_A:
                    k_new, v_new = shared_kv(x, kv_norm, w_kv)
                    k_all = jnp.concatenate([k_past.astype(k_new.dtype), k_new], axis=1)
                    v_all = jnp.concatenate([v_past.astype(v_new.dtype), v_new], axis=1)
                j = l - N_A
                lam_init = 0.8 - 0.6 * math.exp(-0.3 * l)
                x = x + diff_attn_layer(x, k_all, v_all, q_pos, k_pos, da_norm[j], da_w_q[j],
                                        da_lq1[j], da_lk1[j], da_lq2[j], da_lk2[j],
                                        da_subln[j], da_w_o[j], lam_init)
            xn = rmsnorm(x, ffn_norm[l])
            if l % 2 == 0:
                x = x + swiglu(xn, mlp_w_gate[l // 2], mlp_w_up[l // 2], mlp_w_down[l // 2])
            else:
                x = x + moe(xn, moe_router[l // 2], moe_w_gate[l // 2], moe_w_up[l // 2], moe_w_down[l // 2])
        y = rmsnorm(x, final_norm)
        return y, jnp.stack(Cs), jnp.stack(ns), jnp.stack(ms), jnp.stack(convs), k_new, v_new

    zC = jnp.zeros((N_A, BATCH, ML_HEADS, ML_HD, ML_HD), f32)
    zn = jnp.zeros((N_A, BATCH, ML_HEADS, ML_HD), f32)
    zm = jnp.zeros((N_A, BATCH, ML_HEADS), f32)
    zconv = jnp.zeros((N_A, BATCH, ML_CONV - 1, ML_INNER), x_prompt.dtype)
    kp0 = jnp.zeros((BATCH, 0, DA_HEADS, 2, DA_DK), x_prompt.dtype)
    vp0 = jnp.zeros((BATCH, 0, DA_HEADS, DA_DV), x_prompt.dtype)
    y_prompt, p_C, p_n, p_m, p_conv, p_k, p_v = run(x_prompt, zC, zn, zm, zconv, kp0, vp0,
                                                    min(ML_CHUNK, x_prompt.shape[1]))

    n_pages = PAST_LEN // PAGE_SIZE
    k_past = cache_k[page_table].reshape(DEC_BATCH, n_pages * PAGE_SIZE, DA_HEADS, 2, DA_DK)
    v_past = cache_v[page_table].reshape(DEC_BATCH, n_pages * PAGE_SIZE, DA_HEADS, DA_DV)
    y_sample, s_C, s_n, s_m, s_conv, s_k, s_v = run(x_sample, state_mlstm_C, state_mlstm_n, state_mlstm_m,
                                                    state_conv, k_past, v_past, x_sample.shape[1])
    return (y_prompt, y_sample, p_C, p_n, p_m, p_conv, p_k, p_v, s_C, s_n, s_m, s_conv, s_k, s_v)
```

```python
import functools
import math

import jax
import jax.numpy as jnp
from jax import lax
from jax.experimental import pallas as pl
from jax.experimental.pallas import tpu as pltpu

F32 = jnp.float32
BF16 = jnp.bfloat16
EPS = 1e-6
TOP_K = 2
LANES = 128
SUBLANES = 8
VMEM_LIMIT_BYTES = 56 * 2**20


def _tile(dim, pref, align):
    t = (min(pref, dim) // align) * align
    while t >= align:
        if dim % t == 0:
            return t
        t -= align
    return dim


def _cparams(*sem):
    return pltpu.CompilerParams(dimension_semantics=sem, vmem_limit_bytes=VMEM_LIMIT_BYTES)


def _dot(a, b):
    return jnp.dot(a, b, preferred_element_type=F32)


def _dot_nt(a, b):
    return lax.dot_general(a, b, (((1,), (1,)), ((), ())), preferred_element_type=F32)


def _dot_tn(a, b):
    return lax.dot_general(a, b, (((0,), (0,)), ((), ())), preferred_element_type=F32)


def _rms(x, g):
    return x * lax.rsqrt(jnp.mean(x * x, axis=-1, keepdims=True) + EPS) * g


def _silu(x):
    return x / (1.0 + jnp.exp(-x))


def _rms_matmul_body(x_ref, g_ref, w_ref, o_ref, xn_ref):
    @pl.when(pl.program_id(1) == 0)
    def _():
        xn_ref[...] = _rms(x_ref[...], g_ref[...]).astype(BF16)

    o_ref[...] = _dot(xn_ref[...], w_ref[...]).astype(o_ref.dtype)


def rms_matmul(x, g, w, out_dtype):
    n, d = x.shape
    f = w.shape[1]
    tm = _tile(n, 1024, SUBLANES)
    tn = _tile(f, 512, LANES)
    return pl.pallas_call(
        _rms_matmul_body,
        grid=(n // tm, f // tn),
        in_specs=[
            pl.BlockSpec((tm, d), lambda i, j: (i, 0)),
            pl.BlockSpec((1, d), lambda i, j: (0, 0)),
            pl.BlockSpec((d, tn), lambda i, j: (0, j)),
        ],
        out_specs=pl.BlockSpec((tm, tn), lambda i, j: (i, j)),
        out_shape=jax.ShapeDtypeStruct((n, f), out_dtype),
        scratch_shapes=[pltpu.VMEM((tm, d), BF16)],
        compiler_params=_cparams("parallel", "arbitrary"),
    )(x, g.reshape(1, d), w)


def _matmul_body(a_ref, w_ref, o_ref):
    o_ref[...] = _dot(a_ref[...], w_ref[...]).astype(o_ref.dtype)


def matmul(a, w, out_dtype, col_block=0):
    n = a.shape[0]
    k, f = w.shape
    tm = _tile(n, 1024, SUBLANES)
    tn = _tile(f, 512, LANES)
    return pl.pallas_call(
        _matmul_body,
        grid=(n // tm, f // tn),
        in_specs=[
            pl.BlockSpec((tm, k), lambda i, j: (i, col_block)),
            pl.BlockSpec((k, tn), lambda i, j: (0, j)),
        ],
        out_specs=pl.BlockSpec((tm, tn), lambda i, j: (i, j)),
        out_shape=jax.ShapeDtypeStruct((n, f), out_dtype),
        compiler_params=_cparams("parallel", "parallel"),
    )(a, w)


def _matmul_res_body(a_ref, w_ref, r_ref, o_ref):
    o_ref[...] = r_ref[...] + _dot(a_ref[...], w_ref[...])


def matmul_res(a, w, res):
    n, k = a.shape
    f = w.shape[1]
    tm = _tile(n, 512, SUBLANES)
    return pl.pallas_call(
        _matmul_res_body,
        grid=(n // tm,),
        in_specs=[
            pl.BlockSpec((tm, k), lambda i: (i, 0)),
            pl.BlockSpec((k, f), lambda i: (0, 0)),
            pl.BlockSpec((tm, f), lambda i: (i, 0)),
        ],
        out_specs=pl.BlockSpec((tm, f), lambda i: (i, 0)),
        out_shape=jax.ShapeDtypeStruct((n, f), F32),
        compiler_params=_cparams("parallel"),
    )(a, w, res)


def _conv_body(x_ref, c0_ref, w_ref, b_ref, xc_ref, cn_ref, buf_ref, *, bb, tb, kc, cc):
    t = pl.program_id(1)
    lo = SUBLANES - (kc - 1)
    c = x_ref.shape[2]
    for s in range(bb):
        @pl.when(t == 0)
        def _():
            buf_ref[s, lo:SUBLANES, :] = c0_ref[s]

        @pl.when(t > 0)
        def _():
            buf_ref[s, lo:SUBLANES, :] = buf_ref[s, tb + lo:tb + SUBLANES, :]

        buf_ref[s, SUBLANES:SUBLANES + tb, :] = x_ref[s].astype(F32)
        for c0 in range(0, c, cc):
            y = b_ref[:, c0:c0 + cc]
            for i in range(kc):
                y = y + buf_ref[s, lo + i:lo + i + tb, c0:c0 + cc] * w_ref[i:i + 1, c0:c0 + cc]
            xc_ref[s, :, c0:c0 + cc] = _silu(y).astype(BF16)
        cn_ref[s] = buf_ref[s, tb + lo:tb + SUBLANES, :]


def conv_silu(up, conv0, w, b):
    bsz, t, c2 = up.shape
    c = c2 // 2
    kc = w.shape[0]
    tb = _tile(t, 512, SUBLANES)
    bb = _tile(bsz, max(1, 64 // tb), 1) if tb == t else 1
    cc = _tile(c, 512, LANES)
    body = functools.partial(_conv_body, bb=bb, tb=tb, kc=kc, cc=cc)
    return pl.pallas_call(
        body,
        grid=(bsz // bb, t // tb),
        in_specs=[
            pl.BlockSpec((bb, tb, c), lambda i, j: (i, j, 0)),
            pl.BlockSpec((bb, kc - 1, c), lambda i, j: (i, 0, 0)),
            pl.BlockSpec((kc, c), lambda i, j: (0, 0)),
            pl.BlockSpec((1, c), lambda i, j: (0, 0)),
        ],
        out_specs=[
            pl.BlockSpec((bb, tb, c), lambda i, j: (i, j, 0)),
            pl.BlockSpec((bb, kc - 1, c), lambda i, j: (i, 0, 0)),
        ],
        out_shape=[
            jax.ShapeDtypeStruct((bsz, t, c), BF16),
            jax.ShapeDtypeStruct((bsz, kc - 1, c), F32),
        ],
        scratch_shapes=[pltpu.VMEM((bb, tb + SUBLANES, c), F32)],
        compiler_params=_cparams("parallel", "arbitrary"),
    )(up, conv0, w, b.reshape(1, c))


def _gate_body(qk_ref, v_ref, w_ref, b_ref, o_ref, *, nh):
    kq = qk_ref.shape[1]
    g = _dot(qk_ref[...], w_ref[0:kq, :]) + _dot(v_ref[...], w_ref[kq:, :]) + b_ref[...]
    lane = lax.broadcasted_iota(jnp.int32, g.shape, 1)
    ls = jnp.minimum(g, 0.0) - jnp.log(1.0 + jnp.exp(-jnp.abs(g)))
    o_ref[...] = jnp.where(lane >= nh, ls, g)


def mlstm_gates(qk, v, w_ig, b_ig, w_fg, b_fg):
    n = qk.shape[0]
    nh = w_ig.shape[1]
    kin = w_ig.shape[0]
    w = jnp.zeros((kin, LANES), F32).at[:, :nh].set(w_ig).at[:, nh:2 * nh].set(w_fg).astype(BF16)
    b = jnp.zeros((1, LANES), F32).at[0, :nh].set(b_ig).at[0, nh:2 * nh].set(b_fg)
    tm = _tile(n, 512, SUBLANES)
    return pl.pallas_call(
        functools.partial(_gate_body, nh=nh),
        grid=(n // tm,),
        in_specs=[
            pl.BlockSpec((tm, qk.shape[1]), lambda i: (i, 0)),
            pl.BlockSpec((tm, v.shape[1]), lambda i: (i, 0)),
            pl.BlockSpec((kin, LANES), lambda i: (0, 0)),
            pl.BlockSpec((1, LANES), lambda i: (0, 0)),
        ],
        out_specs=pl.BlockSpec((tm, LANES), lambda i: (i, 0)),
        out_shape=jax.ShapeDtypeStruct((n, LANES), F32),
        compiler_params=_cparams("parallel"),
    )(qk, v, w, b)


def _scan_body(q_ref, k_ref, v_ref, gc_ref, gr_ref, c0_ref, n0_ref, m0_ref, gh_ref,
               hn_ref, c_ref, n_ref, m_ref, *, chunk, scale):
    @pl.when(pl.program_id(2) == 0)
    def _():
        c_ref[...] = c0_ref[...]
        n_ref[...] = n0_ref[...]
        m_ref[...] = m0_ref[...]

    q = q_ref[...]
    k = k_ref[...]
    v = v_ref[...]
    ig_c = gc_ref[:, 0:1]
    lf_c = gc_ref[:, 1:2]
    ig_r = gr_ref[0:1, :]
    lf_r = gr_ref[1:2, :]
    row = lax.broadcasted_iota(jnp.int32, (chunk, chunk), 0)
    col = lax.broadcasted_iota(jnp.int32, (chunk, chunk), 1)
    tril = col <= row
    b_c = jnp.sum(jnp.where(tril, lf_r, 0.0), axis=1, keepdims=True)
    b_r = jnp.sum(jnp.where(row <= col, lf_c, 0.0), axis=0, keepdims=True)
    m_prev = m_ref[0:1, 0:1]
    d_log = jnp.where(tril, b_c - b_r + ig_r, -jnp.inf)
    inter = b_c + m_prev
    m_t = jnp.maximum(inter, jnp.max(d_log, axis=1, keepdims=True))
    dw = jnp.exp(d_log - m_t)
    w_inter = jnp.exp(inter - m_t)
    s = _dot_nt(q, k) * (dw * scale)
    num = w_inter * _dot(q, c_ref[...].astype(BF16)) + _dot(s.astype(BF16), v)
    qn = jnp.sum(q.astype(F32) * n_ref[...], axis=1, keepdims=True)
    den = w_inter * qn + jnp.sum(s, axis=1, keepdims=True)
    h = num / jnp.maximum(jnp.abs(den), jnp.exp(-m_t))
    hn_ref[...] = _rms(h, gh_ref[...]).astype(BF16)

    b_last = b_c[chunk - 1:chunk, :]
    m_last = m_t[chunk - 1:chunk, :]
    w_last = jnp.exp(b_last - b_c + ig_c - m_last)
    scale0 = jnp.exp(b_last + m_prev - m_last)
    kw = k.astype(F32) * (w_last * scale)
    c_ref[...] = scale0 * c_ref[...] + _dot_tn(kw.astype(BF16), v)
    n_ref[...] = scale0 * n_ref[...] + jnp.sum(kw, axis=0, keepdims=True)
    m_ref[...] = jnp.broadcast_to(m_last, m_ref.shape)


def mlstm_scan(qk, v, gates, c0, n0, m0, g_head):
    bsz, t, _ = v.shape
    _, nh, dk, dv = c0.shape
    chunk = _tile(t, 256, LANES) if t % LANES == 0 else t
    gi = gates[:, :, :nh]
    gf = gates[:, :, nh:2 * nh]
    g_col = jnp.stack([gi, gf], axis=-1).transpose(0, 2, 1, 3)
    g_row = jnp.stack([gi, gf], axis=-1).transpose(0, 2, 3, 1)
    m0b = jnp.broadcast_to(m0[:, :, None, None], (bsz, nh, 1, LANES))
    body = functools.partial(_scan_body, chunk=chunk, scale=float(dk) ** -0.5)
    hn, c, n, m = pl.pallas_call(
        body,
        grid=(bsz, nh, t // chunk),
        in_specs=[
            pl.BlockSpec((None, chunk, dk), lambda b, h, c: (b, c, h)),
            pl.BlockSpec((None, chunk, dk), lambda b, h, c: (b, c, nh + h)),
            pl.BlockSpec((None, chunk, dv), lambda b, h, c: (b, c, h)),
            pl.BlockSpec((None, None, chunk, 2), lambda b, h, c: (b, h, c, 0)),
            pl.BlockSpec((None, None, 2, chunk), lambda b, h, c: (b, h, 0, c)),
            pl.BlockSpec((None, None, dk, dv), lambda b, h, c: (b, h, 0, 0)),
            pl.BlockSpec((None, None, 1, dk), lambda b, h, c: (b, h, 0, 0)),
            pl.BlockSpec((None, None, 1, LANES), lambda b, h, c: (b, h, 0, 0)),
            pl.BlockSpec((None, 1, dv), lambda b, h, c: (h, 0, 0)),
        ],
        out_specs=[
            pl.BlockSpec((None, chunk, dv), lambda b, h, c: (b, c, h)),
            pl.BlockSpec((None, None, dk, dv), lambda b, h, c: (b, h, 0, 0)),
            pl.BlockSpec((None, None, 1, dk), lambda b, h, c: (b, h, 0, 0)),
            pl.BlockSpec((None, None, 1, LANES), lambda b, h, c: (b, h, 0, 0)),
        ],
        out_shape=[
            jax.ShapeDtypeStruct((bsz, t, nh * dv), BF16),
            jax.ShapeDtypeStruct((bsz, nh, dk, dv), F32),
            jax.ShapeDtypeStruct((bsz, nh, 1, dk), F32),
            jax.ShapeDtypeStruct((bsz, nh, 1, LANES), F32),
        ],
        compiler_params=_cparams("parallel", "parallel", "arbitrary"),
    )(qk, qk, v, g_col, g_row, c0, n0.reshape(bsz, nh, 1, dk), m0b, g_head.reshape(nh, 1, dv))
    return hn, c, n.reshape(bsz, nh, dk), m[:, :, 0, 0]


def _down_body(hn_ref, xc_ref, z_ref, skip_ref, w_ref, r_ref, o_ref):
    z = z_ref[...].astype(F32)
    a = (hn_ref[...].astype(F32) + skip_ref[...] * xc_ref[...].astype(F32)) * _silu(z)
    o_ref[...] = r_ref[...] + _dot(a.astype(BF16), w_ref[...])


def gated_down(hn, xc, up, skip, w, res):
    n, c = hn.shape
    d = w.shape[1]
    tm = _tile(n, 512, SUBLANES)
    return pl.pallas_call(
        _down_body,
        grid=(n // tm,),
        in_specs=[
            pl.BlockSpec((tm, c), lambda i: (i, 0)),
            pl.BlockSpec((tm, c), lambda i: (i, 0)),
            pl.BlockSpec((tm, c), lambda i: (i, 1)),
            pl.BlockSpec((1, c), lambda i: (0, 0)),
            pl.BlockSpec((c, d), lambda i: (0, 0)),
            pl.BlockSpec((tm, d), lambda i: (i, 0)),
        ],
        out_specs=pl.BlockSpec((tm, d), lambda i: (i, 0)),
        out_shape=jax.ShapeDtypeStruct((n, d), F32),
        compiler_params=_cparams("parallel"),
    )(hn, xc, up, skip.reshape(1, c), w, res)


def _ffn_body(x_ref, g_ref, wg_ref, wu_ref, wd_ref, o_ref, xn_ref):
    @pl.when(pl.program_id(1) == 0)
    def _():
        x = x_ref[...]
        xn_ref[...] = _rms(x, g_ref[...]).astype(BF16)
        o_ref[...] = x

    xn = xn_ref[...]
    hmid = _silu(_dot(xn, wg_ref[...])) * _dot(xn, wu_ref[...])
    o_ref[...] += _dot(hmid.astype(BF16), wd_ref[...])


def ffn_dense(x, g, wg, wu, wd):
    n, d = x.shape
    f = wg.shape[1]
    tm = _tile(n, 512, SUBLANES)
    tf = _tile(f, 1408, LANES)
    return pl.pallas_call(
        _ffn_body,
        grid=(n // tm, f // tf),
        in_specs=[
            pl.BlockSpec((tm, d), lambda i, j: (i, 0)),
            pl.BlockSpec((1, d), lambda i, j: (0, 0)),
            pl.BlockSpec((d, tf), lambda i, j: (0, j)),
            pl.BlockSpec((d, tf), lambda i, j: (0, j)),
            pl.BlockSpec((tf, d), lambda i, j: (j, 0)),
        ],
        out_specs=pl.BlockSpec((tm, d), lambda i, j: (i, 0)),
        out_shape=jax.ShapeDtypeStruct((n, d), F32),
        scratch_shapes=[pltpu.VMEM((tm, d), BF16)],
        compiler_params=_cparams("parallel", "arbitrary"),
    )(x, g.reshape(1, d), wg, wu, wd)


def _kv_body(x_ref, g_ref, w_ref, k32_ref, v32_ref, kb_ref, vb_ref):
    xn = _rms(x_ref[...], g_ref[...]).astype(BF16)
    kw = k32_ref.shape[1]
    k = _dot(xn, w_ref[:, :kw])
    v = _dot(xn, w_ref[:, kw:])
    k32_ref[...] = k
    v32_ref[...] = v
    kb_ref[...] = k.astype(BF16)
    vb_ref[...] = v.astype(BF16)


def shared_kv(x, g, w, kw):
    n, d = x.shape
    vw = w.shape[1] - kw
    tm = _tile(n, 512, SUBLANES)
    return pl.pallas_call(
        _kv_body,
        grid=(n // tm,),
        in_specs=[
            pl.BlockSpec((tm, d), lambda i: (i, 0)),
            pl.BlockSpec((1, d), lambda i: (0, 0)),
            pl.BlockSpec((d, kw + vw), lambda i: (0, 0)),
        ],
        out_specs=[
            pl.BlockSpec((tm, kw), lambda i: (i, 0)),
            pl.BlockSpec((tm, vw), lambda i: (i, 0)),
            pl.BlockSpec((tm, kw), lambda i: (i, 0)),
            pl.BlockSpec((tm, vw), lambda i: (i, 0)),
        ],
        out_shape=[
            jax.ShapeDtypeStruct((n, kw), F32),
            jax.ShapeDtypeStruct((n, vw), F32),
            jax.ShapeDtypeStruct((n, kw), BF16),
            jax.ShapeDtypeStruct((n, vw), BF16),
        ],
        compiler_params=_cparams("parallel"),
    )(x, g.reshape(1, d), w)


def _lambda_full(lq1_ref, lk1_ref, lq2_ref, lk2_ref, lam_init):
    a = jnp.sum(lq1_ref[...] * lk1_ref[...], axis=1, keepdims=True)
    b = jnp.sum(lq2_ref[...] * lk2_ref[...], axis=1, keepdims=True)
    return jnp.exp(a) - jnp.exp(b) + lam_init


def _attn_prefill_body(slopes_ref, q_ref, k_ref, v_ref, lq1_ref, lk1_ref, lq2_ref, lk2_ref, gs_ref,
                       o_ref, *, tq, dk, lam_init):
    h = pl.program_id(1)
    i = pl.program_id(2)
    slope = slopes_ref[h]
    q = q_ref[...]
    lane = lax.broadcasted_iota(jnp.int32, q.shape, 1)
    zero = jnp.zeros_like(q)
    q2 = jnp.concatenate([jnp.where(lane < dk, q, zero), jnp.where(lane >= dk, q, zero)], axis=0)
    r = lax.broadcasted_iota(jnp.int32, (2 * tq, tq), 0)
    r = jnp.where(r >= tq, r - tq, r)
    c = lax.broadcasted_iota(jnp.int32, (2 * tq, tq), 1)
    rel = r - c
    base = (-slope) * rel.astype(F32)
    dv = v_ref.shape[1]

    def update(carry, t, cj, kv_start):
        m, l, acc = carry
        vb = v_ref[pl.ds(kv_start, tq), :]
        m_new = jnp.maximum(m, jnp.max(t, axis=1, keepdims=True) + cj)
        p = jnp.exp(t - (m_new - cj))
        alpha = jnp.exp(m - m_new)
        l = alpha * l + jnp.sum(p, axis=1, keepdims=True)
        acc = alpha * acc + _dot(p.astype(BF16), vb)
        return m_new, l, acc

    def step(j, carry):
        kv_start = pl.multiple_of(j * tq, tq)
        kb = k_ref[pl.ds(kv_start, tq), :]
        t = _dot_nt(q2, kb) + base
        cj = (-slope) * ((i - j) * tq).astype(F32)
        return update(carry, t, cj, kv_start)

    init = (jnp.full((2 * tq, 1), -jnp.inf, F32), jnp.zeros((2 * tq, 1), F32), jnp.zeros((2 * tq, dv), F32))
    carry = lax.fori_loop(0, i, step, init)
    kv_start = pl.multiple_of(i * tq, tq)
    kb = k_ref[pl.ds(kv_start, tq), :]
    t = jnp.where(rel >= 0, _dot_nt(q2, kb) + base, -jnp.inf)
    _, l, acc = update(carry, t, 0.0, kv_start)

    o = acc / l
    lam = _lambda_full(lq1_ref, lk1_ref, lq2_ref, lk2_ref, lam_init)
    o = o[:tq] - lam * o[tq:]
    o_ref[...] = (_rms(o, gs_ref[...]) * (1.0 - lam_init)).astype(BF16)


def attn_prefill(q, k, v, slopes, lq1, lk1, lq2, lk2, g_sub, nh, lam_init):
    bsz, t, qw = q.shape
    dk2 = qw // nh
    dv = v.shape[2] // nh
    tq = _tile(t, 256, LANES)
    body = functools.partial(_attn_prefill_body, tq=tq, dk=dk2 // 2, lam_init=lam_init)
    vec = lambda a: a.reshape(1, -1)
    small = lambda n: pl.BlockSpec((1, n), lambda b, h, i: (0, 0))
    return pl.pallas_call(
        body,
        grid=(bsz, nh, t // tq),
        in_specs=[
            pl.BlockSpec(memory_space=pltpu.SMEM),
            pl.BlockSpec((None, tq, dk2), lambda b, h, i: (b, i, h)),
            pl.BlockSpec((None, t, dk2), lambda b, h, i: (b, 0, h)),
            pl.BlockSpec((None, t, dv), lambda b, h, i: (b, 0, h)),
            small(dk2 // 2), small(dk2 // 2), small(dk2 // 2), small(dk2 // 2), small(dv),
        ],
        out_specs=pl.BlockSpec((None, tq, dv), lambda b, h, i: (b, i, h)),
        out_shape=jax.ShapeDtypeStruct((bsz, t, nh * dv), BF16),
        compiler_params=_cparams("parallel", "parallel", "arbitrary"),
    )(slopes, q, k, v, vec(lq1), vec(lk1), vec(lq2), vec(lk2), vec(g_sub))


def _attn_paged_body(pt_ref, q_ref, *refs, pp, nh, dk, dv, t, page, past, lam_init):
    kp_refs = refs[:pp]
    vp_refs = refs[pp:2 * pp]
    (kn_ref, vn_ref, lq1_ref, lk1_ref, lq2_ref, lk2_ref, gs_ref,
     o_ref, qbd_ref, m_ref, l_ref, acc_ref) = refs[2 * pp:]
    j = pl.program_id(1)
    nj = pl.num_programs(1)
    rows = nh * 2 * t
    qw = nh * 2 * dk

    @pl.when(j == 0)
    def _():
        qf = q_ref[...].astype(F32)
        qrep = jnp.concatenate([qf] * (2 * nh), axis=0)
        rr = lax.broadcasted_iota(jnp.int32, (rows, qw), 0)
        cc = lax.broadcasted_iota(jnp.int32, (rows, qw), 1)
        qbd_ref[...] = jnp.where(rr // t == cc // dk, qrep, 0.0).astype(BF16)
        m_ref[...] = jnp.full(m_ref.shape, -jnp.inf, F32)
        l_ref[...] = jnp.zeros(l_ref.shape, F32)
        acc_ref[...] = jnp.zeros(acc_ref.shape, F32)

    ri = lax.broadcasted_iota(jnp.int32, (rows, 1), 0)
    head = (ri // (2 * t)).astype(F32)
    slope = jnp.exp2(-8.0 * (head + 1.0) / nh)
    qpos = past + ri % t
    kl = lax.broadcasted_iota(jnp.int32, (rows, page), 1)

    def update(kb, vb, kpos, valid):
        s = _dot_nt(qbd_ref[...], kb)
        dist = qpos - kpos
        s = s - slope * dist.astype(F32)
        if valid is not None:
            s = jnp.where(valid & (dist >= 0), s, -jnp.inf)
        m = m_ref[...]
        m_new = jnp.maximum(m, jnp.max(s, axis=1, keepdims=True))
        p = jnp.exp(s - m_new)
        alpha = jnp.exp(m - m_new)
        l_ref[...] = alpha * l_ref[...] + jnp.sum(p, axis=1, keepdims=True)
        acc_ref[...] = alpha * acc_ref[...] + _dot(p.astype(BF16), vb)
        m_ref[...] = m_new

    for r in range(pp):
        kpos = (j * pp + r) * page + kl
        update(kp_refs[r][...].astype(BF16), vp_refs[r][...].astype(BF16), kpos, None)

    @pl.when(j == nj - 1)
    def _():
        update(kn_ref[...], vn_ref[...], past + kl, kl < t)
        o = acc_ref[...] / l_ref[...]
        lam = _lambda_full(lq1_ref, lk1_ref, lq2_ref, lk2_ref, lam_init)
        for hh in range(nh):
            r0 = hh * 2 * t
            o1 = o[r0:r0 + t, hh * dv:(hh + 1) * dv]
            o2 = o[r0 + t:r0 + 2 * t, hh * dv:(hh + 1) * dv]
            od = o1 - lam * o2
            o_ref[:, hh * dv:(hh + 1) * dv] = (_rms(od, gs_ref[...]) * (1.0 - lam_init)).astype(BF16)


def attn_paged(q, cache_k, cache_v, page_table, k_new, v_new, lq1, lk1, lq2, lk2, g_sub, nh, lam_init):
    bsz, t, qw = q.shape
    page = cache_k.shape[1]
    vw = cache_v.shape[2]
    dv = vw // nh
    dk = qw // (2 * nh)
    n_pages = page_table.shape[1]
    pp = 4 if n_pages % 4 == 0 else (2 if n_pages % 2 == 0 else 1)
    rows = nh * 2 * t
    kn = jnp.pad(k_new, ((0, 0), (0, page - t), (0, 0)))
    vn = jnp.pad(v_new, ((0, 0), (0, page - t), (0, 0)))
    body = functools.partial(_attn_paged_body, pp=pp, nh=nh, dk=dk, dv=dv, t=t, page=page,
                             past=n_pages * page, lam_init=lam_init)
    vec = lambda a: a.reshape(1, -1)
    small = lambda n: pl.BlockSpec((1, n), lambda b, j, pt: (0, 0))

    def page_spec(width, r):
        return pl.BlockSpec((None, page, width), lambda b, j, pt: (pt[b, j * pp + r], 0, 0))

    grid_spec = pltpu.PrefetchScalarGridSpec(
        num_scalar_prefetch=1,
        grid=(bsz, n_pages // pp),
        in_specs=(
            [pl.BlockSpec((None, t, qw), lambda b, j, pt: (b, 0, 0))]
            + [page_spec(qw, r) for r in range(pp)]
            + [page_spec(vw, r) for r in range(pp)]
            + [pl.BlockSpec((None, page, qw), lambda b, j, pt: (b, 0, 0)),
               pl.BlockSpec((None, page, vw), lambda b, j, pt: (b, 0, 0)),
               small(dk), small(dk), small(dk), small(dk), small(dv)]
        ),
        out_specs=pl.BlockSpec((None, t, vw), lambda b, j, pt: (b, 0, 0)),
        scratch_shapes=[
            pltpu.VMEM((rows, qw), BF16),
            pltpu.VMEM((rows, 1), F32),
            pltpu.VMEM((rows, 1), F32),
            pltpu.VMEM((rows, vw), F32),
        ],
    )
    return pl.pallas_call(
        body,
        grid_spec=grid_spec,
        out_shape=jax.ShapeDtypeStruct((bsz, t, vw), BF16),
        compiler_params=_cparams("parallel", "arbitrary"),
    )(page_table, q, *([cache_k] * pp), *([cache_v] * pp), kn, vn,
      vec(lq1), vec(lk1), vec(lq2), vec(lk2), vec(g_sub))


def _router_body(x_ref, g_ref, wr_ref, xn_ref, idx_ref, gate_ref, *, ne):
    xn = _rms(x_ref[...], g_ref[...])
    xn_ref[...] = xn
    logits = jnp.dot(xn, wr_ref[...], preferred_element_type=F32, precision=lax.Precision.HIGHEST)
    lane = lax.broadcasted_iota(jnp.int32, logits.shape, 1)
    logits = jnp.where(lane < ne, logits, -jnp.inf)
    m1 = jnp.max(logits, axis=1, keepdims=True)
    i1 = jnp.min(jnp.where(logits == m1, lane, LANES), axis=1, keepdims=True)
    rest = jnp.where(lane == i1, -jnp.inf, logits)
    m2 = jnp.max(rest, axis=1, keepdims=True)
    i2 = jnp.min(jnp.where(rest == m2, lane, LANES), axis=1, keepdims=True)
    e = jnp.exp(m2 - m1)
    g1 = 1.0 / (1.0 + e)
    g2 = e / (1.0 + e)
    idx_ref[...] = jnp.where(lane == 0, i1, jnp.where(lane == 1, i2, 0))
    gate_ref[...] = jnp.where(lane == 0, g1, jnp.where(lane == 1, g2, 0.0))


def moe_router(x, g, w_router):
    n, d = x.shape
    ne = w_router.shape[1]
    wr = jnp.zeros((d, LANES), F32).at[:, :ne].set(w_router)
    tm = _tile(n, 512, SUBLANES)
    return pl.pallas_call(
        functools.partial(_router_body, ne=ne),
        grid=(n // tm,),
        in_specs=[
            pl.BlockSpec((tm, d), lambda i: (i, 0)),
            pl.BlockSpec((1, d), lambda i: (0, 0)),
            pl.BlockSpec((d, LANES), lambda i: (0, 0)),
        ],
        out_specs=[
            pl.BlockSpec((tm, d), lambda i: (i, 0)),
            pl.BlockSpec((tm, LANES), lambda i: (i, 0)),
            pl.BlockSpec((tm, LANES), lambda i: (i, 0)),
        ],
        out_shape=[
            jax.ShapeDtypeStruct((n, d), F32),
            jax.ShapeDtypeStruct((n, LANES), jnp.int32),
            jax.ShapeDtypeStruct((n, LANES), F32),
        ],
        compiler_params=_cparams("parallel"),
    )(x, g.reshape(1, d), wr)


def _gather_body(tok_ref, x_hbm, o_ref, sem, *, tg):
    base = pl.program_id(0) * tg

    def issue(r, carry):
        pltpu.make_async_copy(x_hbm.at[pl.ds(tok_ref[base + r], 1)], o_ref.at[pl.ds(r, 1)], sem).start()
        return carry

    lax.fori_loop(0, tg, issue, 0)
    pltpu.make_async_copy(x_hbm.at[pl.ds(0, tg)], o_ref, sem).wait()


def gather_rows(x, tok, tg):
    n, d = x.shape
    r = tok.shape[0]
    grid_spec = pltpu.PrefetchScalarGridSpec(
        num_scalar_prefetch=1,
        grid=(r // tg,),
        in_specs=[pl.BlockSpec(memory_space=pl.ANY)],
        out_specs=pl.BlockSpec((tg, d), lambda i, tok: (i, 0)),
        scratch_shapes=[pltpu.SemaphoreType.DMA(())],
    )
    return pl.pallas_call(
        functools.partial(_gather_body, tg=tg),
        grid_spec=grid_spec,
        out_shape=jax.ShapeDtypeStruct((r, d), x.dtype),
        compiler_params=pltpu.CompilerParams(dimension_semantics=("arbitrary",),
                                             vmem_limit_bytes=VMEM_LIMIT_BYTES),
    )(tok, x)


def _moe_ffn_body(te_ref, tv_ref, xs_ref, wg_ref, wu_ref, wd_ref, o_ref, xb_ref):
    i = pl.program_id(0)
    j = pl.program_id(1)

    @pl.when(tv_ref[i] == 0)
    def _():
        @pl.when(j == 0)
        def _():
            o_ref[...] = jnp.zeros(o_ref.shape, F32)

    @pl.when(tv_ref[i] > 0)
    def _():
        @pl.when(j == 0)
        def _():
            xb_ref[...] = xs_ref[...].astype(BF16)

        xb = xb_ref[...]
        hmid = _silu(_dot(xb, wg_ref[...])) * _dot(xb, wu_ref[...])
        y = _dot(hmid.astype(BF16), wd_ref[...])

        @pl.when(j == 0)
        def _():
            o_ref[...] = y

        @pl.when(j > 0)
        def _():
            o_ref[...] += y


def moe_ffn(xs, tile_expert, tile_valid, wg, wu, wd, tm):
    r, d = xs.shape
    f = wg.shape[2]
    tf = _tile(f, 1408, LANES)
    grid_spec = pltpu.PrefetchScalarGridSpec(
        num_scalar_prefetch=2,
        grid=(r // tm, f // tf),
        in_specs=[
            pl.BlockSpec((tm, d), lambda i, j, te, tv: (i, 0)),
            pl.BlockSpec((None, d, tf), lambda i, j, te, tv: (te[i], 0, j)),
            pl.BlockSpec((None, d, tf), lambda i, j, te, tv: (te[i], 0, j)),
            pl.BlockSpec((None, tf, d), lambda i, j, te, tv: (te[i], j, 0)),
        ],
        out_specs=pl.BlockSpec((tm, d), lambda i, j, te, tv: (i, 0)),
        scratch_shapes=[pltpu.VMEM((tm, d), BF16)],
    )
    return pl.pallas_call(
        _moe_ffn_body,
        grid_spec=grid_spec,
        out_shape=jax.ShapeDtypeStruct((r, d), F32),
        compiler_params=_cparams("parallel", "arbitrary"),
    )(tile_expert, tile_valid, xs, wg, wu, wd)


def _combine_body(pos_ref, x_ref, gate_ref, gf_ref, ys_hbm, o_ref, buf_ref, sem, *, tc):
    base = pl.program_id(0) * tc

    def issue(r, carry):
        for kk in range(TOP_K):
            p = pos_ref[TOP_K * (base + r) + kk]
            pltpu.make_async_copy(ys_hbm.at[pl.ds(p, 1)], buf_ref.at[kk, pl.ds(r, 1)], sem).start()
        return carry

    lax.fori_loop(0, tc, issue, 0)
    for kk in range(TOP_K):
        pltpu.make_async_copy(ys_hbm.at[pl.ds(0, tc)], buf_ref.at[kk], sem).wait()
    gate = gate_ref[...]
    y = x_ref[...]
    for kk in range(TOP_K):
        y = y + gate[:, kk:kk + 1] * buf_ref[kk]
    o_ref[...] = _rms(y, gf_ref[...])


def moe_combine_norm(x, gates, pos, ys, g_final):
    n, d = x.shape
    tc = _tile(n, 256, SUBLANES)
    grid_spec = pltpu.PrefetchScalarGridSpec(
        num_scalar_prefetch=1,
        grid=(n // tc,),
        in_specs=[
            pl.BlockSpec((tc, d), lambda i, pos: (i, 0)),
            pl.BlockSpec((tc, LANES), lambda i, pos: (i, 0)),
            pl.BlockSpec((1, d), lambda i, pos: (0, 0)),
            pl.BlockSpec(memory_space=pl.ANY),
        ],
        out_specs=pl.BlockSpec((tc, d), lambda i, pos: (i, 0)),
        scratch_shapes=[pltpu.VMEM((TOP_K, tc, d), F32), pltpu.SemaphoreType.DMA(())],
    )
    return pl.pallas_call(
        functools.partial(_combine_body, tc=tc),
        grid_spec=grid_spec,
        out_shape=jax.ShapeDtypeStruct((n, d), F32),
        compiler_params=pltpu.CompilerParams(dimension_semantics=("arbitrary",),
                                             vmem_limit_bytes=VMEM_LIMIT_BYTES),
    )(pos, x, gates, g_final.reshape(1, d), ys)


def _routing_tables(top_i, ne, tm):
    n = top_i.shape[0]
    e_flat = top_i.reshape(-1)
    onehot = (e_flat[:, None] == jnp.arange(ne, dtype=jnp.int32)[None, :]).astype(jnp.int32)
    csum = jnp.cumsum(onehot, axis=0)
    rank = jnp.sum(onehot * csum, axis=1) - 1
    counts = csum[-1]
    padded = ((counts + tm - 1) // tm) * tm
    pad_end = jnp.cumsum(padded)
    pad_start = pad_end - padded
    cnt_start = jnp.cumsum(counts) - counts
    pos = pad_start[e_flat] + rank
    n_tiles = (n * TOP_K) // tm + ne
    tile_row0 = jnp.arange(n_tiles, dtype=jnp.int32) * tm
    tile_valid = (tile_row0 < pad_end[-1]).astype(jnp.int32)
    last_e = jnp.max(jnp.where(counts > 0, jnp.arange(ne, dtype=jnp.int32), 0))
    tile_expert = jnp.minimum(jnp.searchsorted(pad_end, tile_row0, side="right").astype(jnp.int32), last_e)
    order = jnp.argsort(e_flat, stable=True).astype(jnp.int32)
    rows = jnp.arange(n_tiles * tm, dtype=jnp.int32)
    row_e = tile_expert[rows // tm]
    row_rank = rows - pad_start[row_e]
    row_ok = (row_rank < counts[row_e]) & (tile_valid[rows // tm] > 0)
    src_pair = order[jnp.clip(cnt_start[row_e] + row_rank, 0, n * TOP_K - 1)]
    row_token = jnp.where(row_ok, src_pair // TOP_K, 0).astype(jnp.int32)
    return row_token, pos.astype(jnp.int32), tile_expert, tile_valid


def moe_block_final(x, g_ffn, w_router, wg, wu, wd, g_final):
    n, d = x.shape
    ne = w_router.shape[1]
    tm = _tile(n * TOP_K, 512, SUBLANES)
    xn, idx, gates = moe_router(x, g_ffn, w_router)
    row_token, pos, tile_expert, tile_valid = _routing_tables(idx[:, :TOP_K], ne, tm)
    xs = gather_rows(xn, row_token, _tile(row_token.shape[0], 256, SUBLANES))
    ys = moe_ffn(xs, tile_expert, tile_valid, wg, wu, wd, tm)
    return moe_combine_norm(x, gates, pos, ys, g_final)


def _run_group(x, c0, n0, m0, conv0, past, w):
    bsz, t, d = x.shape
    n = bsz * t
    nh_ml = c0.shape[1]
    xf = x.reshape(n, d)

    up = rms_matmul(xf, w["ml_norm"], w["ml_w_up"], BF16)
    inner = up.shape[1] // 2
    xc, conv_new = conv_silu(up.reshape(bsz, t, 2 * inner), conv0, w["ml_conv_w"], w["ml_conv_b"])
    xc = xc.reshape(n, inner)
    qk = matmul(xc, w["ml_w_qk"], BF16)
    v = matmul(up, w["ml_w_v"], BF16, col_block=0)
    gates = mlstm_gates(qk, v, w["ml_w_ig"], w["ml_b_ig"], w["ml_w_fg"], w["ml_b_fg"])
    hn, c_new, n_new, m_new = mlstm_scan(qk.reshape(bsz, t, 2 * inner), v.reshape(bsz, t, inner),
                                         gates.reshape(bsz, t, LANES), c0, n0, m0, w["ml_head_norm"])
    x1 = gated_down(hn.reshape(n, inner), xc, up, w["ml_skip"], w["ml_w_down"], xf)
    x2 = ffn_dense(x1, w["ffn_norm0"], w["mlp_w_gate"], w["mlp_w_up"], w["mlp_w_down"])

    nh_da = w["da_heads"]
    kw = w["da_w_q"].shape[1]
    k32, v32, kb, vb = shared_kv(x2, w["kv_norm"], w["w_kv"], kw)
    vw = v32.shape[1]

    q = rms_matmul(x2, w["da_norm"], w["da_w_q"], BF16).reshape(bsz, t, kw)
    lam_init = w["lam_init"]
    lam_args = (w["da_lq1"], w["da_lk1"], w["da_lq2"], w["da_lk2"], w["da_subln"])
    if past is None:
        o = attn_prefill(q, kb.reshape(bsz, t, kw), vb.reshape(bsz, t, vw), w["slopes"], *lam_args, nh_da, lam_init)
    else:
        cache_k, cache_v, page_table = past
        o = attn_paged(q, cache_k, cache_v, page_table, kb.reshape(bsz, t, kw), vb.reshape(bsz, t, vw),
                       *lam_args, nh_da, lam_init)
    x3 = matmul_res(o.reshape(n, vw), w["da_w_o"], x2)
    y = moe_block_final(x3, w["ffn_norm1"], w["moe_router"], w["moe_w_gate"], w["moe_w_up"], w["moe_w_down"],
                        w["final_norm"])
    dk = kw // (2 * nh_da)
    return (y.reshape(bsz, t, d), c_new[None], n_new[None], m_new[None], conv_new[None],
            k32.reshape(bsz, t, nh_da, 2, dk), v32.reshape(bsz, t, nh_da, vw // nh_da))


def kernel(x_prompt, x_sample, state_mlstm_C, state_mlstm_n, state_mlstm_m, state_conv, cache_k, cache_v, page_table, ml_norm, ml_w_up, ml_conv_w, ml_conv_b, ml_w_q, ml_w_k, ml_w_v, ml_w_ig, ml_b_ig, ml_w_fg, ml_b_fg, ml_head_norm, ml_skip, ml_w_down, kv_norm, w_kv, da_norm, da_w_q, da_lq1, da_lk1, da_lq2, da_lk2, da_subln, da_w_o, ffn_norm, mlp_w_gate, mlp_w_up, mlp_w_down, moe_router, moe_w_gate, moe_w_up, moe_w_down, final_norm):
    assert ml_norm.shape[0] == 1 and da_norm.shape[0] == 1 and ffn_norm.shape[0] == 2, "one mLSTM layer then one attention layer"
    nh_da = cache_k.shape[2]
    dk = cache_k.shape[4]
    layer = 1
    w = dict(
        ml_norm=ml_norm[0], ml_w_up=ml_w_up[0].astype(BF16), ml_conv_w=ml_conv_w[0], ml_conv_b=ml_conv_b[0],
        ml_w_qk=jnp.concatenate([ml_w_q[0], ml_w_k[0]], axis=1).astype(BF16), ml_w_v=ml_w_v[0].astype(BF16),
        ml_w_ig=ml_w_ig[0], ml_b_ig=ml_b_ig[0], ml_w_fg=ml_w_fg[0], ml_b_fg=ml_b_fg[0],
        ml_head_norm=ml_head_norm[0], ml_skip=ml_skip[0], ml_w_down=ml_w_down[0].astype(BF16),
        kv_norm=kv_norm, w_kv=w_kv.astype(BF16), da_norm=da_norm[0],
        da_w_q=(da_w_q[0] * (float(dk) ** -0.5)).astype(BF16),
        da_lq1=da_lq1[0], da_lk1=da_lk1[0], da_lq2=da_lq2[0], da_lk2=da_lk2[0], da_subln=da_subln[0],
        da_w_o=da_w_o[0].astype(BF16), ffn_norm0=ffn_norm[0], ffn_norm1=ffn_norm[1],
        mlp_w_gate=mlp_w_gate[0].astype(BF16), mlp_w_up=mlp_w_up[0].astype(BF16), mlp_w_down=mlp_w_down[0].astype(BF16),
        moe_router=moe_router[0], moe_w_gate=moe_w_gate[0].astype(BF16), moe_w_up=moe_w_up[0].astype(BF16),
        moe_w_down=moe_w_down[0].astype(BF16), final_norm=final_norm,
        da_heads=nh_da, lam_init=0.8 - 0.6 * math.exp(-0.3 * layer),
        slopes=jnp.exp2(-8.0 * jnp.arange(1, nh_da + 1, dtype=F32) / nh_da),
    )
    bp = x_prompt.shape[0]
    _, _, nh_ml, hd, _ = state_mlstm_C.shape
    kc = state_conv.shape[2]
    inner = state_conv.shape[3]
    zeros = lambda *s: jnp.zeros(s, F32)
    out_p = _run_group(x_prompt, zeros(bp, nh_ml, hd, hd), zeros(bp, nh_ml, hd), zeros(bp, nh_ml),
                       zeros(bp, kc, inner), None, w)
    pool, page = cache_k.shape[0], cache_k.shape[1]
    past = (cache_k.reshape(pool, page, -1), cache_v.reshape(pool, page, -1), page_table)
    out_s = _run_group(x_sample, state_mlstm_C[0], state_mlstm_n[0], state_mlstm_m[0], state_conv[0], past, w)
    y_p, p_c, p_n, p_m, p_conv, p_k, p_v = out_p
    y_s, s_c, s_n, s_m, s_conv, s_k, s_v = out_s
    return (y_p, y_s, p_c, p_n, p_m, p_conv, p_k, p_v, s_c, s_n, s_m, s_conv, s_k, s_v)
```

```python
import functools
import math

import jax
import jax.numpy as jnp
from jax import lax
from jax.experimental import pallas as pl
from jax.experimental.pallas import tpu as pltpu

F32 = jnp.float32
BF16 = jnp.bfloat16
EPS = 1e-6
TOP_K = 2
LOG2E = 1.4426950408889634
LANES = 128
SUBLANES = 8
VMEM_LIMIT_BYTES = 56 * 2**20


def _tile(dim, pref, align):
    t = (min(pref, dim) // align) * align
    while t >= align:
        if dim % t == 0:
            return t
        t -= align
    return dim


def _cparams(*sem):
    return pltpu.CompilerParams(dimension_semantics=sem, vmem_limit_bytes=VMEM_LIMIT_BYTES)


def _dot(a, b):
    return jnp.dot(a, b, preferred_element_type=F32)


def _dot_nt(a, b):
    return lax.dot_general(a, b, (((1,), (1,)), ((), ())), preferred_element_type=F32)


def _dot_tn(a, b):
    return lax.dot_general(a, b, (((0,), (0,)), ((), ())), preferred_element_type=F32)


def _rms(x, g):
    return x * lax.rsqrt(jnp.mean(x * x, axis=-1, keepdims=True) + EPS) * g


def _silu(x):
    return x / (1.0 + jnp.exp(-x))


def _rms_matmul_body(x_ref, g_ref, w_ref, o_ref, xn_ref):
    @pl.when(pl.program_id(1) == 0)
    def _():
        xn_ref[...] = _rms(x_ref[...], g_ref[...]).astype(BF16)

    o_ref[...] = _dot(xn_ref[...], w_ref[...]).astype(o_ref.dtype)


def rms_matmul(x, g, w, out_dtype):
    n, d = x.shape
    f = w.shape[1]
    tm = _tile(n, 1024, SUBLANES)
    tn = _tile(f, 512, LANES)
    return pl.pallas_call(
        _rms_matmul_body,
        grid=(n // tm, f // tn),
        in_specs=[
            pl.BlockSpec((tm, d), lambda i, j: (i, 0)),
            pl.BlockSpec((1, d), lambda i, j: (0, 0)),
            pl.BlockSpec((d, tn), lambda i, j: (0, j)),
        ],
        out_specs=pl.BlockSpec((tm, tn), lambda i, j: (i, j)),
        out_shape=jax.ShapeDtypeStruct((n, f), out_dtype),
        scratch_shapes=[pltpu.VMEM((tm, d), BF16)],
        compiler_params=_cparams("parallel", "arbitrary"),
    )(x, g.reshape(1, d), w)


def _matmul_body(a_ref, w_ref, o_ref):
    o_ref[...] = _dot(a_ref[...], w_ref[...]).astype(o_ref.dtype)


def matmul(a, w, out_dtype, col_block=0):
    n = a.shape[0]
    k, f = w.shape
    tm = _tile(n, 1024, SUBLANES)
    tn = _tile(f, 512, LANES)
    return pl.pallas_call(
        _matmul_body,
        grid=(n // tm, f // tn),
        in_specs=[
            pl.BlockSpec((tm, k), lambda i, j: (i, col_block)),
            pl.BlockSpec((k, tn), lambda i, j: (0, j)),
        ],
        out_specs=pl.BlockSpec((tm, tn), lambda i, j: (i, j)),
        out_shape=jax.ShapeDtypeStruct((n, f), out_dtype),
        compiler_params=_cparams("parallel", "parallel"),
    )(a, w)


def _matmul_res_body(a_ref, w_ref, r_ref, o_ref):
    o_ref[...] = r_ref[...] + _dot(a_ref[...], w_ref[...])


def matmul_res(a, w, res):
    n, k = a.shape
    f = w.shape[1]
    tm = _tile(n, 512, SUBLANES)
    return pl.pallas_call(
        _matmul_res_body,
        grid=(n // tm,),
        in_specs=[
            pl.BlockSpec((tm, k), lambda i: (i, 0)),
            pl.BlockSpec((k, f), lambda i: (0, 0)),
            pl.BlockSpec((tm, f), lambda i: (i, 0)),
        ],
        out_specs=pl.BlockSpec((tm, f), lambda i: (i, 0)),
        out_shape=jax.ShapeDtypeStruct((n, f), F32),
        compiler_params=_cparams("parallel"),
    )(a, w, res)


def _conv_body(x_ref, c0_ref, w_ref, b_ref, xc_ref, cn_ref, buf_ref, *, bb, tb, kc, cc):
    t = pl.program_id(1)
    lo = SUBLANES - (kc - 1)
    c = x_ref.shape[2]
    for s in range(bb):
        @pl.when(t == 0)
        def _():
            buf_ref[s, lo:SUBLANES, :] = c0_ref[s]

        @pl.when(t > 0)
        def _():
            buf_ref[s, lo:SUBLANES, :] = buf_ref[s, tb + lo:tb + SUBLANES, :]

        buf_ref[s, SUBLANES:SUBLANES + tb, :] = x_ref[s].astype(F32)
        for c0 in range(0, c, cc):
            y = b_ref[:, c0:c0 + cc]
            for i in range(kc):
                y = y + buf_ref[s, lo + i:lo + i + tb, c0:c0 + cc] * w_ref[i:i + 1, c0:c0 + cc]
            xc_ref[s, :, c0:c0 + cc] = _silu(y).astype(BF16)
        cn_ref[s] = buf_ref[s, tb + lo:tb + SUBLANES, :]


def conv_silu(up, conv0, w, b):
    bsz, t, c2 = up.shape
    c = c2 // 2
    kc = w.shape[0]
    tb = _tile(t, 512, SUBLANES)
    bb = _tile(bsz, max(1, 64 // tb), 1) if tb == t else 1
    cc = _tile(c, 512, LANES)
    body = functools.partial(_conv_body, bb=bb, tb=tb, kc=kc, cc=cc)
    return pl.pallas_call(
        body,
        grid=(bsz // bb, t // tb),
        in_specs=[
            pl.BlockSpec((bb, tb, c), lambda i, j: (i, j, 0)),
            pl.BlockSpec((bb, kc - 1, c), lambda i, j: (i, 0, 0)),
            pl.BlockSpec((kc, c), lambda i, j: (0, 0)),
            pl.BlockSpec((1, c), lambda i, j: (0, 0)),
        ],
        out_specs=[
            pl.BlockSpec((bb, tb, c), lambda i, j: (i, j, 0)),
            pl.BlockSpec((bb, kc - 1, c), lambda i, j: (i, 0, 0)),
        ],
        out_shape=[
            jax.ShapeDtypeStruct((bsz, t, c), BF16),
            jax.ShapeDtypeStruct((bsz, kc - 1, c), F32),
        ],
        scratch_shapes=[pltpu.VMEM((bb, tb + SUBLANES, c), F32)],
        compiler_params=_cparams("parallel", "arbitrary"),
    )(up, conv0, w, b.reshape(1, c))


def _gate_body(qk_ref, v_ref, w_ref, b_ref, o_ref, *, nh):
    kq = qk_ref.shape[1]
    g = _dot(qk_ref[...], w_ref[0:kq, :]) + _dot(v_ref[...], w_ref[kq:, :]) + b_ref[...]
    lane = lax.broadcasted_iota(jnp.int32, g.shape, 1)
    ls = jnp.minimum(g, 0.0) - jnp.log(1.0 + jnp.exp(-jnp.abs(g)))
    o_ref[...] = jnp.where(lane >= nh, ls, g)


def mlstm_gates(qk, v, w_ig, b_ig, w_fg, b_fg):
    n = qk.shape[0]
    nh = w_ig.shape[1]
    kin = w_ig.shape[0]
    w = jnp.zeros((kin, LANES), F32).at[:, :nh].set(w_ig).at[:, nh:2 * nh].set(w_fg).astype(BF16)
    b = jnp.zeros((1, LANES), F32).at[0, :nh].set(b_ig).at[0, nh:2 * nh].set(b_fg)
    tm = _tile(n, 512, SUBLANES)
    return pl.pallas_call(
        functools.partial(_gate_body, nh=nh),
        grid=(n // tm,),
        in_specs=[
            pl.BlockSpec((tm, qk.shape[1]), lambda i: (i, 0)),
            pl.BlockSpec((tm, v.shape[1]), lambda i: (i, 0)),
            pl.BlockSpec((kin, LANES), lambda i: (0, 0)),
            pl.BlockSpec((1, LANES), lambda i: (0, 0)),
        ],
        out_specs=pl.BlockSpec((tm, LANES), lambda i: (i, 0)),
        out_shape=jax.ShapeDtypeStruct((n, LANES), F32),
        compiler_params=_cparams("parallel"),
    )(qk, v, w, b)


def _scan_body(q_ref, k_ref, v_ref, gc_ref, gr_ref, c0_ref, n0_ref, m0_ref, gh_ref,
               hn_ref, c_ref, n_ref, m_ref, *, chunk, scale, hb, dk, dv):
    @pl.when(pl.program_id(2) == 0)
    def _():
        c_ref[...] = c0_ref[...]
        n_ref[...] = n0_ref[...]
        m_ref[...] = m0_ref[...]

    row = lax.broadcasted_iota(jnp.int32, (chunk, chunk), 0)
    col = lax.broadcasted_iota(jnp.int32, (chunk, chunk), 1)
    tril = col <= row
    for hh in range(hb):
        q = q_ref[:, hh * dk:(hh + 1) * dk]
        k = k_ref[:, hh * dk:(hh + 1) * dk]
        v = v_ref[:, hh * dv:(hh + 1) * dv]
        ig_c = gc_ref[hh, :, 0:1]
        lf_c = gc_ref[hh, :, 1:2]
        ig_r = gr_ref[hh, 0:1, :]
        lf_r = gr_ref[hh, 1:2, :]
        b_c = jnp.sum(jnp.where(tril, lf_r, 0.0), axis=1, keepdims=True)
        b_r = jnp.sum(jnp.where(row <= col, lf_c, 0.0), axis=0, keepdims=True)
        m_prev = m_ref[hh, 0:1, 0:1]
        d_log = jnp.where(tril, b_c - b_r + ig_r, -jnp.inf)
        inter = b_c + m_prev
        m_t = jnp.maximum(inter, jnp.max(d_log, axis=1, keepdims=True))
        dw = jnp.exp(d_log - m_t)
        w_inter = jnp.exp(inter - m_t)
        s = _dot_nt(q, k) * (dw * scale)
        num = w_inter * _dot(q, c_ref[hh].astype(BF16)) + _dot(s.astype(BF16), v)
        qn = jnp.sum(q.astype(F32) * n_ref[hh], axis=1, keepdims=True)
        den = w_inter * qn + jnp.sum(s, axis=1, keepdims=True)
        h = num / jnp.maximum(jnp.abs(den), jnp.exp(-m_t))
        hn_ref[:, hh * dv:(hh + 1) * dv] = _rms(h, gh_ref[hh]).astype(BF16)

        b_last = b_c[chunk - 1:chunk, :]
        m_last = m_t[chunk - 1:chunk, :]
        w_last = jnp.exp(b_last - b_c + ig_c - m_last)
        scale0 = jnp.exp(b_last + m_prev - m_last)
        kw = k.astype(F32) * (w_last * scale)
        c_ref[hh] = scale0 * c_ref[hh] + _dot_tn(kw.astype(BF16), v)
        n_ref[hh] = scale0 * n_ref[hh] + jnp.sum(kw, axis=0, keepdims=True)
        m_ref[hh] = jnp.broadcast_to(m_last, (1, LANES))


def mlstm_scan(qk, v, gates, c0, n0, m0, g_head):
    bsz, t, _ = v.shape
    _, nh, dk, dv = c0.shape
    chunk = _tile(t, 256, LANES) if t % LANES == 0 else t
    gi = gates[:, :, :nh]
    gf = gates[:, :, nh:2 * nh]
    g_col = jnp.stack([gi, gf], axis=-1).transpose(0, 2, 1, 3)
    g_row = jnp.stack([gi, gf], axis=-1).transpose(0, 2, 3, 1)
    m0b = jnp.broadcast_to(m0[:, :, None, None], (bsz, nh, 1, LANES))
    hb = nh if t == chunk else 1
    ng = nh // hb
    body = functools.partial(_scan_body, chunk=chunk, scale=float(dk) ** -0.5, hb=hb, dk=dk, dv=dv)
    hn, c, n, m = pl.pallas_call(
        body,
        grid=(bsz, ng, t // chunk),
        in_specs=[
            pl.BlockSpec((None, chunk, hb * dk), lambda b, h, c: (b, c, h)),
            pl.BlockSpec((None, chunk, hb * dk), lambda b, h, c: (b, c, ng + h)),
            pl.BlockSpec((None, chunk, hb * dv), lambda b, h, c: (b, c, h)),
            pl.BlockSpec((None, hb, chunk, 2), lambda b, h, c: (b, h, c, 0)),
            pl.BlockSpec((None, hb, 2, chunk), lambda b, h, c: (b, h, 0, c)),
            pl.BlockSpec((None, hb, dk, dv), lambda b, h, c: (b, h, 0, 0)),
            pl.BlockSpec((None, hb, 1, dk), lambda b, h, c: (b, h, 0, 0)),
            pl.BlockSpec((None, hb, 1, LANES), lambda b, h, c: (b, h, 0, 0)),
            pl.BlockSpec((hb, 1, dv), lambda b, h, c: (h, 0, 0)),
        ],
        out_specs=[
            pl.BlockSpec((None, chunk, hb * dv), lambda b, h, c: (b, c, h)),
            pl.BlockSpec((None, hb, dk, dv), lambda b, h, c: (b, h, 0, 0)),
            pl.BlockSpec((None, hb, 1, dk), lambda b, h, c: (b, h, 0, 0)),
            pl.BlockSpec((None, hb, 1, LANES), lambda b, h, c: (b, h, 0, 0)),
        ],
        out_shape=[
            jax.ShapeDtypeStruct((bsz, t, nh * dv), BF16),
            jax.ShapeDtypeStruct((bsz, nh, dk, dv), F32),
            jax.ShapeDtypeStruct((bsz, nh, 1, dk), F32),
            jax.ShapeDtypeStruct((bsz, nh, 1, LANES), F32),
        ],
        compiler_params=_cparams("parallel", "parallel", "arbitrary"),
    )(qk, qk, v, g_col, g_row, c0, n0.reshape(bsz, nh, 1, dk), m0b, g_head.reshape(nh, 1, dv))
    return hn, c, n.reshape(bsz, nh, dk), m[:, :, 0, 0]


def _down_body(hn_ref, xc_ref, z_ref, skip_ref, w_ref, r_ref, o_ref):
    z = z_ref[...].astype(F32)
    a = (hn_ref[...].astype(F32) + skip_ref[...] * xc_ref[...].astype(F32)) * _silu(z)
    o_ref[...] = r_ref[...] + _dot(a.astype(BF16), w_ref[...])


def gated_down(hn, xc, up, skip, w, res):
    n, c = hn.shape
    d = w.shape[1]
    tm = _tile(n, 512, SUBLANES)
    return pl.pallas_call(
        _down_body,
        grid=(n // tm,),
        in_specs=[
            pl.BlockSpec((tm, c), lambda i: (i, 0)),
            pl.BlockSpec((tm, c), lambda i: (i, 0)),
            pl.BlockSpec((tm, c), lambda i: (i, 1)),
            pl.BlockSpec((1, c), lambda i: (0, 0)),
            pl.BlockSpec((c, d), lambda i: (0, 0)),
            pl.BlockSpec((tm, d), lambda i: (i, 0)),
        ],
        out_specs=pl.BlockSpec((tm, d), lambda i: (i, 0)),
        out_shape=jax.ShapeDtypeStruct((n, d), F32),
        compiler_params=_cparams("parallel"),
    )(hn, xc, up, skip.reshape(1, c), w, res)


def _ffn_body(x_ref, g_ref, wg_ref, wu_ref, wd_ref, o_ref, xn_ref):
    @pl.when(pl.program_id(1) == 0)
    def _():
        x = x_ref[...]
        xn_ref[...] = _rms(x, g_ref[...]).astype(BF16)
        o_ref[...] = x

    xn = xn_ref[...]
    hmid = _silu(_dot(xn, wg_ref[...])) * _dot(xn, wu_ref[...])
    o_ref[...] += _dot(hmid.astype(BF16), wd_ref[...])


def ffn_dense(x, g, wg, wu, wd):
    n, d = x.shape
    f = wg.shape[1]
    tm = _tile(n, 512, SUBLANES)
    tf = _tile(f, 1408, LANES)
    return pl.pallas_call(
        _ffn_body,
        grid=(n // tm, f // tf),
        in_specs=[
            pl.BlockSpec((tm, d), lambda i, j: (i, 0)),
            pl.BlockSpec((1, d), lambda i, j: (0, 0)),
            pl.BlockSpec((d, tf), lambda i, j: (0, j)),
            pl.BlockSpec((d, tf), lambda i, j: (0, j)),
            pl.BlockSpec((tf, d), lambda i, j: (j, 0)),
        ],
        out_specs=pl.BlockSpec((tm, d), lambda i, j: (i, 0)),
        out_shape=jax.ShapeDtypeStruct((n, d), F32),
        scratch_shapes=[pltpu.VMEM((tm, d), BF16)],
        compiler_params=_cparams("parallel", "arbitrary"),
    )(x, g.reshape(1, d), wg, wu, wd)


def _kv_body(x_ref, g_ref, w_ref, k32_ref, v32_ref, kb_ref, vb_ref):
    xn = _rms(x_ref[...], g_ref[...]).astype(BF16)
    kw = k32_ref.shape[1]
    k = _dot(xn, w_ref[:, :kw])
    v = _dot(xn, w_ref[:, kw:])
    k32_ref[...] = k
    v32_ref[...] = v
    kb_ref[...] = k.astype(BF16)
    vb_ref[...] = v.astype(BF16)


def shared_kv(x, g, w, kw):
    n, d = x.shape
    vw = w.shape[1] - kw
    tm = _tile(n, 512, SUBLANES)
    return pl.pallas_call(
        _kv_body,
        grid=(n // tm,),
        in_specs=[
            pl.BlockSpec((tm, d), lambda i: (i, 0)),
            pl.BlockSpec((1, d), lambda i: (0, 0)),
            pl.BlockSpec((d, kw + vw), lambda i: (0, 0)),
        ],
        out_specs=[
            pl.BlockSpec((tm, kw), lambda i: (i, 0)),
            pl.BlockSpec((tm, vw), lambda i: (i, 0)),
            pl.BlockSpec((tm, kw), lambda i: (i, 0)),
            pl.BlockSpec((tm, vw), lambda i: (i, 0)),
        ],
        out_shape=[
            jax.ShapeDtypeStruct((n, kw), F32),
            jax.ShapeDtypeStruct((n, vw), F32),
            jax.ShapeDtypeStruct((n, kw), BF16),
            jax.ShapeDtypeStruct((n, vw), BF16),
        ],
        compiler_params=_cparams("parallel"),
    )(x, g.reshape(1, d), w)


def _lambda_full(lq1_ref, lk1_ref, lq2_ref, lk2_ref, lam_init):
    a = jnp.sum(lq1_ref[...] * lk1_ref[...], axis=1, keepdims=True)
    b = jnp.sum(lq2_ref[...] * lk2_ref[...], axis=1, keepdims=True)
    return jnp.exp(a) - jnp.exp(b) + lam_init


def _attn_prefill_body(slopes_ref, q_ref, k_ref, v_ref, lq1_ref, lk1_ref, lq2_ref, lk2_ref, gs_ref,
                       o_ref, t_ref, m_ref, al_ref, l_ref, acc_ref, *, tq, dk, lam_init):
    h = pl.program_id(1)
    i = pl.program_id(2)
    nt = tq // LANES
    slope = slopes_ref[h] * LOG2E
    q = q_ref[...]
    lane = lax.broadcasted_iota(jnp.int32, q.shape, 1)
    zero = jnp.zeros_like(q)
    qz = (jnp.where(lane < dk, q, zero), jnp.where(lane >= dk, q, zero))
    r = lax.broadcasted_iota(jnp.int32, (tq, tq), 0)
    c = lax.broadcasted_iota(jnp.int32, (tq, tq), 1)
    rel = r - c
    base = (-slope) * rel.astype(F32)
    m_ref[...] = jnp.full(m_ref.shape, -jnp.inf, F32)
    l_ref[...] = jnp.zeros(l_ref.shape, F32)
    acc_ref[...] = jnp.zeros(acc_ref.shape, F32)

    def block_bias(j):
        return (-slope) * ((i - j) * tq).astype(F32)

    def scores(j, masked):
        kb = k_ref[pl.ds(pl.multiple_of(j * tq, tq), tq), :]
        cj = block_bias(j)
        for mp in range(2):
            t = _dot_nt(qz[mp], kb) + base
            if masked:
                t = jnp.where(rel >= 0, t, -jnp.inf)
            t_ref[mp] = t
            tm = t[:, 0:LANES]
            for ct in range(1, nt):
                tm = jnp.maximum(tm, t[:, ct * LANES:(ct + 1) * LANES])
            m_old = m_ref[mp]
            m_new = jnp.maximum(m_old, jnp.max(tm, axis=1, keepdims=True) + cj)
            al_ref[mp] = jnp.exp2(m_old - m_new)
            m_ref[mp] = m_new

    def accumulate(j):
        vb = v_ref[pl.ds(pl.multiple_of(j * tq, tq), tq), :]
        cj = block_bias(j)
        for mp in range(2):
            mb = m_ref[mp] - cj
            alpha = al_ref[mp]
            ps = []
            lsum = None
            for ct in range(nt):
                pc = jnp.exp2(t_ref[mp, :, ct * LANES:(ct + 1) * LANES] - mb)
                lsum = pc if lsum is None else lsum + pc
                ps.append(pc.astype(BF16))
            l_ref[mp] = alpha * l_ref[mp] + lsum
            a_acc = alpha if acc_ref.shape[2] == LANES else alpha[:, 0:1]
            acc_ref[mp] = a_acc * acc_ref[mp] + _dot(jnp.concatenate(ps, axis=1), vb)

    @pl.when(i == 0)
    def _():
        scores(0, True)

    @pl.when(i > 0)
    def _():
        scores(0, False)

    def step(j, carry):
        accumulate(j)
        scores(j + 1, False)
        return carry

    lax.fori_loop(0, i - 1, step, 0)

    @pl.when(i > 0)
    def _():
        accumulate(i - 1)
        scores(i, True)

    accumulate(i)

    lam = _lambda_full(lq1_ref, lk1_ref, lq2_ref, lk2_ref, lam_init)
    l0 = jnp.sum(l_ref[0], axis=1, keepdims=True)
    l1 = jnp.sum(l_ref[1], axis=1, keepdims=True)
    o = acc_ref[0] / l0 - lam * (acc_ref[1] / l1)
    o_ref[...] = (_rms(o, gs_ref[...]) * (1.0 - lam_init)).astype(BF16)


def attn_prefill(q, k, v, slopes, lq1, lk1, lq2, lk2, g_sub, nh, lam_init):
    bsz, t, qw = q.shape
    dk2 = qw // nh
    dv = v.shape[2] // nh
    tq = _tile(t, 512, LANES)
    body = functools.partial(_attn_prefill_body, tq=tq, dk=dk2 // 2, lam_init=lam_init)
    vec = lambda a: a.reshape(1, -1)
    small = lambda n: pl.BlockSpec((1, n), lambda b, h, i: (0, 0))
    return pl.pallas_call(
        body,
        grid=(bsz, nh, t // tq),
        in_specs=[
            pl.BlockSpec(memory_space=pltpu.SMEM),
            pl.BlockSpec((None, tq, dk2), lambda b, h, i: (b, i, h)),
            pl.BlockSpec((None, t, dk2), lambda b, h, i: (b, 0, h)),
            pl.BlockSpec((None, t, dv), lambda b, h, i: (b, 0, h)),
            small(dk2 // 2), small(dk2 // 2), small(dk2 // 2), small(dk2 // 2), small(dv),
        ],
        out_specs=pl.BlockSpec((None, tq, dv), lambda b, h, i: (b, i, h)),
        out_shape=jax.ShapeDtypeStruct((bsz, t, nh * dv), BF16),
        scratch_shapes=[pltpu.VMEM((2, tq, tq), F32), pltpu.VMEM((2, tq, LANES), F32), pltpu.VMEM((2, tq, LANES), F32),
                        pltpu.VMEM((2, tq, LANES), F32), pltpu.VMEM((2, tq, dv), F32)],
        compiler_params=_cparams("parallel", "parallel", "arbitrary"),
    )(slopes, q, k, v, vec(lq1), vec(lk1), vec(lq2), vec(lk2), vec(g_sub))


def _attn_paged_body(pt_ref, q_ref, *refs, pp, nh, dk, dv, t, page, past, lam_init):
    kp_refs = refs[:pp]
    vp_refs = refs[pp:2 * pp]
    (kn_ref, vn_ref, lq1_ref, lk1_ref, lq2_ref, lk2_ref, gs_ref,
     o_ref, qbd_ref, m_ref, l_ref, acc_ref) = refs[2 * pp:]
    j = pl.program_id(1)
    nj = pl.num_programs(1)
    rows = nh * 2 * t
    hr = 2 * t
    qw = nh * 2 * dk

    @pl.when(j == 0)
    def _():
        qf = q_ref[...].astype(F32)
        qrep = jnp.concatenate([qf] * (2 * nh), axis=0)
        rr = lax.broadcasted_iota(jnp.int32, (rows, qw), 0)
        cc = lax.broadcasted_iota(jnp.int32, (rows, qw), 1)
        qbd_ref[...] = jnp.where(rr // t == cc // dk, qrep, 0.0).astype(BF16)
        m_ref[...] = jnp.full(m_ref.shape, -jnp.inf, F32)
        l_ref[...] = jnp.zeros(l_ref.shape, F32)
        acc_ref[...] = jnp.zeros(acc_ref.shape, F32)

    ri = lax.broadcasted_iota(jnp.int32, (rows, 1), 0)
    head = (ri // hr).astype(F32)
    slope = jnp.exp2(-8.0 * (head + 1.0) / nh) * LOG2E
    qpos = past + ri % t
    kl = lax.broadcasted_iota(jnp.int32, (rows, page), 1)

    def update(blocks):
        ss = []
        for s, _, kpos, valid in blocks:
            dist = qpos - kpos
            s = s - slope * dist.astype(F32)
            if valid is not None:
                s = jnp.where(valid & (dist >= 0), s, -jnp.inf)
            ss.append(s)
        s = jnp.concatenate(ss, axis=1)
        m = m_ref[...]
        m_new = jnp.maximum(m, jnp.max(s, axis=1, keepdims=True))
        p = jnp.exp2(s - m_new)
        alpha = jnp.exp2(m - m_new)
        l_ref[...] = alpha * l_ref[...] + jnp.sum(p, axis=1, keepdims=True)
        m_ref[...] = m_new
        pb = p.astype(BF16)
        for hh in range(nh):
            rs = slice(hh * hr, (hh + 1) * hr)
            vh = jnp.concatenate([blk[1](hh) for blk in blocks], axis=0)
            acc_ref[rs, :] = alpha[rs] * acc_ref[rs, :] + _dot(pb[rs], vh)

    def past_blocks():
        out = []
        for r in range(pp):
            s = _dot(qbd_ref[...], kp_refs[r][...].astype(BF16))
            v_head = lambda hh, r=r: vp_refs[r][pl.ds(hh, page, stride=nh), :].astype(BF16)
            out.append((s, v_head, (j * pp + r) * page + kl, None))
        return out

    @pl.when(j < nj - 1)
    def _():
        update(past_blocks())

    @pl.when(j == nj - 1)
    def _():
        new = (_dot_nt(qbd_ref[...], kn_ref[...]), lambda hh: vn_ref[:, hh * dv:(hh + 1) * dv], past + kl, kl < t)
        update(past_blocks() + [new])
        o = acc_ref[...] / l_ref[...]
        lam = _lambda_full(lq1_ref, lk1_ref, lq2_ref, lk2_ref, lam_init)
        for hh in range(nh):
            r0 = hh * hr
            od = o[r0:r0 + t] - lam * o[r0 + t:r0 + hr]
            o_ref[:, hh * dv:(hh + 1) * dv] = (_rms(od, gs_ref[...]) * (1.0 - lam_init)).astype(BF16)


def attn_paged(q, cache_kt, cache_v, page_table, k_new, v_new, lq1, lk1, lq2, lk2, g_sub, lam_init):
    bsz, t, qw = q.shape
    _, page, nh, dv = cache_v.shape
    vw = nh * dv
    dk = qw // (2 * nh)
    n_pages = page_table.shape[1]
    pp = 4 if n_pages % 4 == 0 else (2 if n_pages % 2 == 0 else 1)
    rows = nh * 2 * t
    kn = jnp.pad(k_new, ((0, 0), (0, page - t), (0, 0)))
    vn = jnp.pad(v_new, ((0, 0), (0, page - t), (0, 0)))
    body = functools.partial(_attn_paged_body, pp=pp, nh=nh, dk=dk, dv=dv, t=t, page=page,
                             past=n_pages * page, lam_init=lam_init)
    vec = lambda a: a.reshape(1, -1)
    small = lambda n: pl.BlockSpec((1, n), lambda b, j, pt: (0, 0))
    k_spec = lambda r: pl.BlockSpec((None, qw, page), lambda b, j, pt: (pt[b, j * pp + r], 0, 0))
    v_spec = lambda r: pl.BlockSpec((None, page * nh, dv), lambda b, j, pt: (pt[b, j * pp + r], 0, 0))
    grid_spec = pltpu.PrefetchScalarGridSpec(
        num_scalar_prefetch=1,
        grid=(bsz, n_pages // pp),
        in_specs=(
            [pl.BlockSpec((None, t, qw), lambda b, j, pt: (b, 0, 0))]
            + [k_spec(r) for r in range(pp)]
            + [v_spec(r) for r in range(pp)]
            + [pl.BlockSpec((None, page, qw), lambda b, j, pt: (b, 0, 0)),
               pl.BlockSpec((None, page, vw), lambda b, j, pt: (b, 0, 0)),
               small(dk), small(dk), small(dk), small(dk), small(dv)]
        ),
        out_specs=pl.BlockSpec((None, t, vw), lambda b, j, pt: (b, 0, 0)),
        scratch_shapes=[
            pltpu.VMEM((rows, qw), BF16),
            pltpu.VMEM((rows, 1), F32),
            pltpu.VMEM((rows, 1), F32),
            pltpu.VMEM((rows, dv), F32),
        ],
    )
    return pl.pallas_call(
        body,
        grid_spec=grid_spec,
        out_shape=jax.ShapeDtypeStruct((bsz, t, vw), BF16),
        compiler_params=_cparams("parallel", "arbitrary"),
    )(page_table, q, *([cache_kt] * pp), *([cache_v.reshape(-1, page * nh, dv)] * pp), kn, vn,
      vec(lq1), vec(lk1), vec(lq2), vec(lk2), vec(g_sub))


def _router_body(x_ref, g_ref, wr_ref, xn_ref, idx_ref, gate_ref, *, ne):
    xn = _rms(x_ref[...], g_ref[...])
    xn_ref[...] = xn
    logits = jnp.dot(xn, wr_ref[...], preferred_element_type=F32, precision=lax.Precision.HIGHEST)
    lane = lax.broadcasted_iota(jnp.int32, logits.shape, 1)
    logits = jnp.where(lane < ne, logits, -jnp.inf)
    m1 = jnp.max(logits, axis=1, keepdims=True)
    i1 = jnp.min(jnp.where(logits == m1, lane, LANES), axis=1, keepdims=True)
    rest = jnp.where(lane == i1, -jnp.inf, logits)
    m2 = jnp.max(rest, axis=1, keepdims=True)
    i2 = jnp.min(jnp.where(rest == m2, lane, LANES), axis=1, keepdims=True)
    e = jnp.exp(m2 - m1)
    g1 = 1.0 / (1.0 + e)
    g2 = e / (1.0 + e)
    idx_ref[...] = jnp.where(lane == 0, i1, jnp.where(lane == 1, i2, 0))
    gate_ref[...] = jnp.where(lane == 0, g1, jnp.where(lane == 1, g2, 0.0))


def moe_router(x, g, w_router):
    n, d = x.shape
    ne = w_router.shape[1]
    wr = jnp.zeros((d, LANES), F32).at[:, :ne].set(w_router)
    tm = _tile(n, 512, SUBLANES)
    return pl.pallas_call(
        functools.partial(_router_body, ne=ne),
        grid=(n // tm,),
        in_specs=[
            pl.BlockSpec((tm, d), lambda i: (i, 0)),
            pl.BlockSpec((1, d), lambda i: (0, 0)),
            pl.BlockSpec((d, LANES), lambda i: (0, 0)),
        ],
        out_specs=[
            pl.BlockSpec((tm, d), lambda i: (i, 0)),
            pl.BlockSpec((tm, LANES), lambda i: (i, 0)),
            pl.BlockSpec((tm, LANES), lambda i: (i, 0)),
        ],
        out_shape=[
            jax.ShapeDtypeStruct((n, d), F32),
            jax.ShapeDtypeStruct((n, LANES), jnp.int32),
            jax.ShapeDtypeStruct((n, LANES), F32),
        ],
        compiler_params=_cparams("parallel"),
    )(x, g.reshape(1, d), wr)


def _dispatch_body(pos_ref, x_ref, xs_init_hbm, xs_hbm, sem, *, tc):
    del xs_init_hbm
    base = pl.program_id(0) * tc

    def issue(r, carry):
        for kk in range(TOP_K):
            p = pos_ref[TOP_K * (base + r) + kk]
            pltpu.make_async_copy(x_ref.at[pl.ds(r, 1)], xs_hbm.at[pl.ds(p, 1)], sem).start()
        return carry

    lax.fori_loop(0, tc, issue, 0)
    for kk in range(TOP_K):
        pltpu.make_async_copy(x_ref, xs_hbm.at[pl.ds(0, tc)], sem).wait()


def dispatch_rows(x, pos, n_rows):
    n, d = x.shape
    tc = _tile(n, 256, SUBLANES)
    grid_spec = pltpu.PrefetchScalarGridSpec(
        num_scalar_prefetch=1,
        grid=(n // tc,),
        in_specs=[pl.BlockSpec((tc, d), lambda i, pos: (i, 0)), pl.BlockSpec(memory_space=pl.ANY)],
        out_specs=pl.BlockSpec(memory_space=pl.ANY),
        scratch_shapes=[pltpu.SemaphoreType.DMA(())],
    )
    return pl.pallas_call(
        functools.partial(_dispatch_body, tc=tc),
        grid_spec=grid_spec,
        out_shape=jax.ShapeDtypeStruct((n_rows, d), x.dtype),
        input_output_aliases={2: 0},
        compiler_params=pltpu.CompilerParams(dimension_semantics=("arbitrary",),
                                             vmem_limit_bytes=VMEM_LIMIT_BYTES),
    )(pos, x, jnp.zeros((n_rows, d), x.dtype))


def _moe_ffn_body(te_ref, tv_ref, xs_ref, wg_ref, wu_ref, wd_ref, o_ref, xb_ref):
    i = pl.program_id(0)
    j = pl.program_id(1)

    @pl.when(tv_ref[i] == 0)
    def _():
        @pl.when(j == 0)
        def _():
            o_ref[...] = jnp.zeros(o_ref.shape, F32)

    @pl.when(tv_ref[i] > 0)
    def _():
        @pl.when(j == 0)
        def _():
            xb_ref[...] = xs_ref[...].astype(BF16)

        xb = xb_ref[...]
        hmid = _silu(_dot(xb, wg_ref[...])) * _dot(xb, wu_ref[...])
        y = _dot(hmid.astype(BF16), wd_ref[...])

        @pl.when(j == 0)
        def _():
            o_ref[...] = y

        @pl.when(j > 0)
        def _():
            o_ref[...] += y


def moe_ffn(xs, tile_expert, tile_valid, wg, wu, wd, tm):
    r, d = xs.shape
    f = wg.shape[2]
    tf = _tile(f, 1408, LANES)
    grid_spec = pltpu.PrefetchScalarGridSpec(
        num_scalar_prefetch=2,
        grid=(r // tm, f // tf),
        in_specs=[
            pl.BlockSpec((tm, d), lambda i, j, te, tv: (i, 0)),
            pl.BlockSpec((None, d, tf), lambda i, j, te, tv: (te[i], 0, j)),
            pl.BlockSpec((None, d, tf), lambda i, j, te, tv: (te[i], 0, j)),
            pl.BlockSpec((None, tf, d), lambda i, j, te, tv: (te[i], j, 0)),
        ],
        out_specs=pl.BlockSpec((tm, d), lambda i, j, te, tv: (i, 0)),
        scratch_shapes=[pltpu.VMEM((tm, d), BF16)],
    )
    return pl.pallas_call(
        _moe_ffn_body,
        grid_spec=grid_spec,
        out_shape=jax.ShapeDtypeStruct((r, d), F32),
        compiler_params=_cparams("parallel", "arbitrary"),
    )(tile_expert, tile_valid, xs, wg, wu, wd)


def _combine_body(pos_ref, x_ref, gate_ref, gf_ref, ys_hbm, o_ref, buf_ref, sem, *, tc):
    base = pl.program_id(0) * tc

    def issue(r, carry):
        for kk in range(TOP_K):
            p = pos_ref[TOP_K * (base + r) + kk]
            pltpu.make_async_copy(ys_hbm.at[pl.ds(p, 1)], buf_ref.at[kk, pl.ds(r, 1)], sem).start()
        return carry

    lax.fori_loop(0, tc, issue, 0)
    for kk in range(TOP_K):
        pltpu.make_async_copy(ys_hbm.at[pl.ds(0, tc)], buf_ref.at[kk], sem).wait()
    gate = gate_ref[...]
    y = x_ref[...]
    for kk in range(TOP_K):
        y = y + gate[:, kk:kk + 1] * buf_ref[kk]
    o_ref[...] = _rms(y, gf_ref[...])


def moe_combine_norm(x, gates, pos, ys, g_final):
    n, d = x.shape
    tc = _tile(n, 256, SUBLANES)
    grid_spec = pltpu.PrefetchScalarGridSpec(
        num_scalar_prefetch=1,
        grid=(n // tc,),
        in_specs=[
            pl.BlockSpec((tc, d), lambda i, pos: (i, 0)),
            pl.BlockSpec((tc, LANES), lambda i, pos: (i, 0)),
            pl.BlockSpec((1, d), lambda i, pos: (0, 0)),
            pl.BlockSpec(memory_space=pl.ANY),
        ],
        out_specs=pl.BlockSpec((tc, d), lambda i, pos: (i, 0)),
        scratch_shapes=[pltpu.VMEM((TOP_K, tc, d), F32), pltpu.SemaphoreType.DMA(())],
    )
    return pl.pallas_call(
        functools.partial(_combine_body, tc=tc),
        grid_spec=grid_spec,
        out_shape=jax.ShapeDtypeStruct((n, d), F32),
        compiler_params=pltpu.CompilerParams(dimension_semantics=("arbitrary",),
                                             vmem_limit_bytes=VMEM_LIMIT_BYTES),
    )(pos, x, gates, g_final.reshape(1, d), ys)


def _routing_tables(top_i, ne, tm):
    n = top_i.shape[0]
    e_flat = top_i.reshape(-1)
    onehot = (e_flat[:, None] == jnp.arange(ne, dtype=jnp.int32)[None, :]).astype(jnp.int32)
    csum = jnp.cumsum(onehot, axis=0)
    rank = jnp.sum(onehot * csum, axis=1) - 1
    counts = csum[-1]
    padded = ((counts + tm - 1) // tm) * tm
    pad_end = jnp.cumsum(padded)
    pad_start = pad_end - padded
    pos = jnp.sum(onehot * pad_start[None, :], axis=1) + rank
    n_tiles = (n * TOP_K) // tm + ne
    tile_row0 = jnp.arange(n_tiles, dtype=jnp.int32) * tm
    tile_valid = (tile_row0 < pad_end[-1]).astype(jnp.int32)
    last_e = jnp.max(jnp.where(counts > 0, jnp.arange(ne, dtype=jnp.int32), 0))
    tile_expert = jnp.sum((pad_end[None, :] <= tile_row0[:, None]).astype(jnp.int32), axis=1)
    tile_expert = jnp.minimum(tile_expert, last_e)
    return pos.astype(jnp.int32), tile_expert.astype(jnp.int32), tile_valid


def moe_block_final(x, g_ffn, w_router, wg, wu, wd, g_final):
    n, d = x.shape
    ne = w_router.shape[1]
    tm = _tile(n * TOP_K, 512, SUBLANES)
    xn, idx, gates = moe_router(x, g_ffn, w_router)
    pos, tile_expert, tile_valid = _routing_tables(idx[:, :TOP_K], ne, tm)
    xs = dispatch_rows(xn, pos, tile_expert.shape[0] * tm)
    ys = moe_ffn(xs, tile_expert, tile_valid, wg, wu, wd, tm)
    return moe_combine_norm(x, gates, pos, ys, g_final)


def _run_group(x, c0, n0, m0, conv0, past, w):
    bsz, t, d = x.shape
    n = bsz * t
    nh_ml = c0.shape[1]
    xf = x.reshape(n, d)

    up = rms_matmul(xf, w["ml_norm"], w["ml_w_up"], BF16)
    inner = up.shape[1] // 2
    xc, conv_new = conv_silu(up.reshape(bsz, t, 2 * inner), conv0, w["ml_conv_w"], w["ml_conv_b"])
    xc = xc.reshape(n, inner)
    qk = matmul(xc, w["ml_w_qk"], BF16)
    v = matmul(up, w["ml_w_v"], BF16, col_block=0)
    gates = mlstm_gates(qk, v, w["ml_w_ig"], w["ml_b_ig"], w["ml_w_fg"], w["ml_b_fg"])
    hn, c_new, n_new, m_new = mlstm_scan(qk.reshape(bsz, t, 2 * inner), v.reshape(bsz, t, inner),
                                         gates.reshape(bsz, t, LANES), c0, n0, m0, w["ml_head_norm"])
    x1 = gated_down(hn.reshape(n, inner), xc, up, w["ml_skip"], w["ml_w_down"], xf)
    x2 = ffn_dense(x1, w["ffn_norm0"], w["mlp_w_gate"], w["mlp_w_up"], w["mlp_w_down"])

    nh_da = w["da_heads"]
    kw = w["da_w_q"].shape[1]
    k32, v32, kb, vb = shared_kv(x2, w["kv_norm"], w["w_kv"], kw)
    vw = v32.shape[1]

    q = rms_matmul(x2, w["da_norm"], w["da_w_q"], BF16).reshape(bsz, t, kw)
    lam_init = w["lam_init"]
    lam_args = (w["da_lq1"], w["da_lk1"], w["da_lq2"], w["da_lk2"], w["da_subln"])
    if past is None:
        o = attn_prefill(q, kb.reshape(bsz, t, kw), vb.reshape(bsz, t, vw), w["slopes"], *lam_args, nh_da, lam_init)
    else:
        cache_kt, cache_v, page_table = past
        o = attn_paged(q, cache_kt, cache_v, page_table, kb.reshape(bsz, t, kw), vb.reshape(bsz, t, vw),
                       *lam_args, lam_init)
    x3 = matmul_res(o.reshape(n, vw), w["da_w_o"], x2)
    y = moe_block_final(x3, w["ffn_norm1"], w["moe_router"], w["moe_w_gate"], w["moe_w_up"], w["moe_w_down"],
                        w["final_norm"])
    dk = kw // (2 * nh_da)
    return (y.reshape(bsz, t, d), c_new[None], n_new[None], m_new[None], conv_new[None],
            k32.reshape(bsz, t, nh_da, 2, dk), v32.reshape(bsz, t, nh_da, vw // nh_da))


def kernel(x_prompt, x_sample, state_mlstm_C, state_mlstm_n, state_mlstm_m, state_conv, cache_k, cache_v, page_table, ml_norm, ml_w_up, ml_conv_w, ml_conv_b, ml_w_q, ml_w_k, ml_w_v, ml_w_ig, ml_b_ig, ml_w_fg, ml_b_fg, ml_head_norm, ml_skip, ml_w_down, kv_norm, w_kv, da_norm, da_w_q, da_lq1, da_lk1, da_lq2, da_lk2, da_subln, da_w_o, ffn_norm, mlp_w_gate, mlp_w_up, mlp_w_down, moe_router, moe_w_gate, moe_w_up, moe_w_down, final_norm):
    assert ml_norm.shape[0] == 1 and da_norm.shape[0] == 1 and ffn_norm.shape[0] == 2, "one mLSTM layer then one attention layer"
    nh_da = cache_k.shape[2]
    dk = cache_k.shape[4]
    layer = 1
    w = dict(
        ml_norm=ml_norm[0], ml_w_up=ml_w_up[0].astype(BF16), ml_conv_w=ml_conv_w[0], ml_conv_b=ml_conv_b[0],
        ml_w_qk=jnp.concatenate([ml_w_q[0], ml_w_k[0]], axis=1).astype(BF16), ml_w_v=ml_w_v[0].astype(BF16),
        ml_w_ig=ml_w_ig[0], ml_b_ig=ml_b_ig[0], ml_w_fg=ml_w_fg[0], ml_b_fg=ml_b_fg[0],
        ml_head_norm=ml_head_norm[0], ml_skip=ml_skip[0], ml_w_down=ml_w_down[0].astype(BF16),
        kv_norm=kv_norm, w_kv=w_kv.astype(BF16), da_norm=da_norm[0],
        da_w_q=(da_w_q[0] * (float(dk) ** -0.5 * LOG2E)).astype(BF16),
        da_lq1=da_lq1[0], da_lk1=da_lk1[0], da_lq2=da_lq2[0], da_lk2=da_lk2[0], da_subln=da_subln[0],
        da_w_o=da_w_o[0].astype(BF16), ffn_norm0=ffn_norm[0], ffn_norm1=ffn_norm[1],
        mlp_w_gate=mlp_w_gate[0].astype(BF16), mlp_w_up=mlp_w_up[0].astype(BF16), mlp_w_down=mlp_w_down[0].astype(BF16),
        moe_router=moe_router[0], moe_w_gate=moe_w_gate[0].astype(BF16), moe_w_up=moe_w_up[0].astype(BF16),
        moe_w_down=moe_w_down[0].astype(BF16), final_norm=final_norm,
        da_heads=nh_da, lam_init=0.8 - 0.6 * math.exp(-0.3 * layer),
        slopes=jnp.exp2(-8.0 * jnp.arange(1, nh_da + 1, dtype=F32) / nh_da),
    )
    bp = x_prompt.shape[0]
    _, _, nh_ml, hd, _ = state_mlstm_C.shape
    kc = state_conv.shape[2]
    inner = state_conv.shape[3]
    zeros = lambda *s: jnp.zeros(s, F32)
    out_p = _run_group(x_prompt, zeros(bp, nh_ml, hd, hd), zeros(bp, nh_ml, hd), zeros(bp, nh_ml),
                       zeros(bp, kc, inner), None, w)
    pool, page = cache_k.shape[0], cache_k.shape[1]
    cache_kt = jnp.transpose(cache_k, (0, 2, 3, 4, 1)).reshape(pool, -1, page)
    past = (cache_kt, cache_v, page_table)
    out_s = _run_group(x_sample, state_mlstm_C[0], state_mlstm_n[0], state_mlstm_m[0], state_conv[0], past, w)
    y_p, p_c, p_n, p_m, p_conv, p_k, p_v = out_p
    y_s, s_c, s_n, s_m, s_conv, s_k, s_v = out_s
    return (y_p, y_s, p_c, p_n, p_m, p_conv, p_k, p_v, s_c, s_n, s_m, s_conv, s_k, s_v)
```

```python
import functools
import math

import jax
import jax.numpy as jnp
from jax import lax
from jax.experimental import pallas as pl
from jax.experimental.pallas import tpu as pltpu

F32 = jnp.float32
BF16 = jnp.bfloat16
EPS = 1e-6
TOP_K = 2
LOG2E = 1.4426950408889634
LANES = 128
SUBLANES = 8
VMEM_LIMIT_BYTES = 56 * 2**20


def _tile(dim, pref, align):
    t = (min(pref, dim) // align) * align
    while t >= align:
        if dim % t == 0:
            return t
        t -= align
    return dim


def _cparams(*sem):
    return pltpu.CompilerParams(dimension_semantics=sem, vmem_limit_bytes=VMEM_LIMIT_BYTES)


def _dot(a, b):
    return jnp.dot(a, b, preferred_element_type=F32)


def _dot_nt(a, b):
    return lax.dot_general(a, b, (((1,), (1,)), ((), ())), preferred_element_type=F32)


def _dot_tn(a, b):
    return lax.dot_general(a, b, (((0,), (0,)), ((), ())), preferred_element_type=F32)


def _rms(x, g):
    return x * lax.rsqrt(jnp.mean(x * x, axis=-1, keepdims=True) + EPS) * g


def _silu(x):
    return x / (1.0 + jnp.exp(-x))


def _rms_matmul_body(x_ref, g_ref, w_ref, o_ref, xn_ref):
    @pl.when(pl.program_id(1) == 0)
    def _():
        xn_ref[...] = _rms(x_ref[...], g_ref[...]).astype(BF16)

    o_ref[...] = _dot(xn_ref[...], w_ref[...]).astype(o_ref.dtype)


def rms_matmul(x, g, w, out_dtype):
    n, d = x.shape
    f = w.shape[1]
    tm = _tile(n, 1024, SUBLANES)
    tn = _tile(f, 1024 if f <= 1024 else 512, LANES)
    return pl.pallas_call(
        _rms_matmul_body,
        grid=(n // tm, f // tn),
        in_specs=[
            pl.BlockSpec((tm, d), lambda i, j: (i, 0)),
            pl.BlockSpec((1, d), lambda i, j: (0, 0)),
            pl.BlockSpec((d, tn), lambda i, j: (0, j)),
        ],
        out_specs=pl.BlockSpec((tm, tn), lambda i, j: (i, j)),
        out_shape=jax.ShapeDtypeStruct((n, f), out_dtype),
        scratch_shapes=[pltpu.VMEM((tm, d), BF16)],
        compiler_params=_cparams("parallel", "arbitrary"),
    )(x, g.reshape(1, d), w)


def _matmul_body(a_ref, w_ref, o_ref):
    o_ref[...] = _dot(a_ref[...], w_ref[...]).astype(o_ref.dtype)


def matmul(a, w, out_dtype, col_block=0):
    n = a.shape[0]
    k, f = w.shape
    tm = _tile(n, 1024, SUBLANES)
    tn = _tile(f, 512, LANES)
    return pl.pallas_call(
        _matmul_body,
        grid=(n // tm, f // tn),
        in_specs=[
            pl.BlockSpec((tm, k), lambda i, j: (i, col_block)),
            pl.BlockSpec((k, tn), lambda i, j: (0, j)),
        ],
        out_specs=pl.BlockSpec((tm, tn), lambda i, j: (i, j)),
        out_shape=jax.ShapeDtypeStruct((n, f), out_dtype),
        compiler_params=_cparams("parallel", "parallel"),
    )(a, w)


def _matmul_res_body(a_ref, w_ref, r_ref, o_ref):
    o_ref[...] = r_ref[...] + _dot(a_ref[...], w_ref[...])


def matmul_res(a, w, res):
    n, k = a.shape
    f = w.shape[1]
    tm = _tile(n, 512, SUBLANES)
    return pl.pallas_call(
        _matmul_res_body,
        grid=(n // tm,),
        in_specs=[
            pl.BlockSpec((tm, k), lambda i: (i, 0)),
            pl.BlockSpec((k, f), lambda i: (0, 0)),
            pl.BlockSpec((tm, f), lambda i: (i, 0)),
        ],
        out_specs=pl.BlockSpec((tm, f), lambda i: (i, 0)),
        out_shape=jax.ShapeDtypeStruct((n, f), F32),
        compiler_params=_cparams("parallel"),
    )(a, w, res)


def _conv_body(x_ref, c0_ref, w_ref, b_ref, xc_ref, cn_ref, buf_ref, *, bb, tb, kc, cc):
    t = pl.program_id(1)
    lo = SUBLANES - (kc - 1)
    c = x_ref.shape[2]
    for s in range(bb):
        @pl.when(t == 0)
        def _():
            buf_ref[s, lo:SUBLANES, :] = c0_ref[s]

        @pl.when(t > 0)
        def _():
            buf_ref[s, lo:SUBLANES, :] = buf_ref[s, tb + lo:tb + SUBLANES, :]

        buf_ref[s, SUBLANES:SUBLANES + tb, :] = x_ref[s].astype(F32)
        for c0 in range(0, c, cc):
            y = b_ref[:, c0:c0 + cc]
            for i in range(kc):
                y = y + buf_ref[s, lo + i:lo + i + tb, c0:c0 + cc] * w_ref[i:i + 1, c0:c0 + cc]
            xc_ref[s, :, c0:c0 + cc] = _silu(y).astype(BF16)
        cn_ref[s] = buf_ref[s, tb + lo:tb + SUBLANES, :]


def conv_silu(up, conv0, w, b):
    bsz, t, c2 = up.shape
    c = c2 // 2
    kc = w.shape[0]
    tb = _tile(t, 512, SUBLANES)
    bb = _tile(bsz, max(1, 64 // tb), 1) if tb == t else 1
    cc = _tile(c, 512, LANES)
    body = functools.partial(_conv_body, bb=bb, tb=tb, kc=kc, cc=cc)
    return pl.pallas_call(
        body,
        grid=(bsz // bb, t // tb),
        in_specs=[
            pl.BlockSpec((bb, tb, c), lambda i, j: (i, j, 0)),
            pl.BlockSpec((bb, kc - 1, c), lambda i, j: (i, 0, 0)),
            pl.BlockSpec((kc, c), lambda i, j: (0, 0)),
            pl.BlockSpec((1, c), lambda i, j: (0, 0)),
        ],
        out_specs=[
            pl.BlockSpec((bb, tb, c), lambda i, j: (i, j, 0)),
            pl.BlockSpec((bb, kc - 1, c), lambda i, j: (i, 0, 0)),
        ],
        out_shape=[
            jax.ShapeDtypeStruct((bsz, t, c), BF16),
            jax.ShapeDtypeStruct((bsz, kc - 1, c), F32),
        ],
        scratch_shapes=[pltpu.VMEM((bb, tb + SUBLANES, c), F32)],
        compiler_params=_cparams("parallel", "arbitrary"),
    )(up, conv0, w, b.reshape(1, c))


def _gate_body(qk_ref, v_ref, w_ref, b_ref, o_ref, *, nh):
    kq = qk_ref.shape[1]
    g = _dot(qk_ref[...], w_ref[0:kq, :]) + _dot(v_ref[...], w_ref[kq:, :]) + b_ref[...]
    lane = lax.broadcasted_iota(jnp.int32, g.shape, 1)
    ls = jnp.minimum(g, 0.0) - jnp.log(1.0 + jnp.exp(-jnp.abs(g)))
    o_ref[...] = jnp.where(lane >= nh, ls, g)


def mlstm_gates(qk, v, w_ig, b_ig, w_fg, b_fg):
    n = qk.shape[0]
    nh = w_ig.shape[1]
    kin = w_ig.shape[0]
    w = jnp.zeros((kin, LANES), F32).at[:, :nh].set(w_ig).at[:, nh:2 * nh].set(w_fg).astype(BF16)
    b = jnp.zeros((1, LANES), F32).at[0, :nh].set(b_ig).at[0, nh:2 * nh].set(b_fg)
    tm = _tile(n, 512, SUBLANES)
    return pl.pallas_call(
        functools.partial(_gate_body, nh=nh),
        grid=(n // tm,),
        in_specs=[
            pl.BlockSpec((tm, qk.shape[1]), lambda i: (i, 0)),
            pl.BlockSpec((tm, v.shape[1]), lambda i: (i, 0)),
            pl.BlockSpec((kin, LANES), lambda i: (0, 0)),
            pl.BlockSpec((1, LANES), lambda i: (0, 0)),
        ],
        out_specs=pl.BlockSpec((tm, LANES), lambda i: (i, 0)),
        out_shape=jax.ShapeDtypeStruct((n, LANES), F32),
        compiler_params=_cparams("parallel"),
    )(qk, v, w, b)


def _scan_body(q_ref, k_ref, v_ref, gc_ref, gr_ref, c0_ref, n0_ref, m0_ref, gh_ref,
               hn_ref, c_ref, n_ref, m_ref, *, chunk, scale, hb, dk, dv):
    @pl.when(pl.program_id(2) == 0)
    def _():
        c_ref[...] = c0_ref[...]
        n_ref[...] = n0_ref[...]
        m_ref[...] = m0_ref[...]

    row = lax.broadcasted_iota(jnp.int32, (chunk, chunk), 0)
    col = lax.broadcasted_iota(jnp.int32, (chunk, chunk), 1)
    tril = col <= row
    for hh in range(hb):
        q = q_ref[:, hh * dk:(hh + 1) * dk]
        k = k_ref[:, hh * dk:(hh + 1) * dk]
        v = v_ref[:, hh * dv:(hh + 1) * dv]
        ig_c = gc_ref[hh, :, 0:1]
        lf_c = gc_ref[hh, :, 1:2]
        ig_r = gr_ref[hh, 0:1, :]
        lf_r = gr_ref[hh, 1:2, :]
        b_c = jnp.sum(jnp.where(tril, lf_r, 0.0), axis=1, keepdims=True)
        b_r = jnp.sum(jnp.where(row <= col, lf_c, 0.0), axis=0, keepdims=True)
        m_prev = m_ref[hh, 0:1, 0:1]
        d_log = jnp.where(tril, b_c - b_r + ig_r, -jnp.inf)
        inter = b_c + m_prev
        m_t = jnp.maximum(inter, jnp.max(d_log, axis=1, keepdims=True))
        dw = jnp.exp(d_log - m_t)
        w_inter = jnp.exp(inter - m_t)
        s = _dot_nt(q, k) * (dw * scale)
        num = w_inter * _dot(q, c_ref[hh].astype(BF16)) + _dot(s.astype(BF16), v)
        qn = jnp.sum(q.astype(F32) * n_ref[hh], axis=1, keepdims=True)
        den = w_inter * qn + jnp.sum(s, axis=1, keepdims=True)
        h = num / jnp.maximum(jnp.abs(den), jnp.exp(-m_t))
        hn_ref[:, hh * dv:(hh + 1) * dv] = _rms(h, gh_ref[hh]).astype(BF16)

        b_last = b_c[chunk - 1:chunk, :]
        m_last = m_t[chunk - 1:chunk, :]
        w_last = jnp.exp(b_last - b_c + ig_c - m_last)
        scale0 = jnp.exp(b_last + m_prev - m_last)
        kw = k.astype(F32) * (w_last * scale)
        c_ref[hh] = scale0 * c_ref[hh] + _dot_tn(kw.astype(BF16), v)
        n_ref[hh] = scale0 * n_ref[hh] + jnp.sum(kw, axis=0, keepdims=True)
        m_ref[hh] = jnp.broadcast_to(m_last, (1, LANES))


def mlstm_scan(qk, v, gates, c0, n0, m0, g_head):
    bsz, t, _ = v.shape
    _, nh, dk, dv = c0.shape
    chunk = _tile(t, 256, LANES) if t % LANES == 0 else t
    gi = gates[:, :, :nh]
    gf = gates[:, :, nh:2 * nh]
    g_col = jnp.stack([gi, gf], axis=-1).transpose(0, 2, 1, 3)
    g_row = jnp.stack([gi, gf], axis=-1).transpose(0, 2, 3, 1)
    m0b = jnp.broadcast_to(m0[:, :, None, None], (bsz, nh, 1, LANES))
    hb = nh if t == chunk else 1
    ng = nh // hb
    body = functools.partial(_scan_body, chunk=chunk, scale=float(dk) ** -0.5, hb=hb, dk=dk, dv=dv)
    hn, c, n, m = pl.pallas_call(
        body,
        grid=(bsz, ng, t // chunk),
        in_specs=[
            pl.BlockSpec((None, chunk, hb * dk), lambda b, h, c: (b, c, h)),
            pl.BlockSpec((None, chunk, hb * dk), lambda b, h, c: (b, c, ng + h)),
            pl.BlockSpec((None, chunk, hb * dv), lambda b, h, c: (b, c, h)),
            pl.BlockSpec((None, hb, chunk, 2), lambda b, h, c: (b, h, c, 0)),
            pl.BlockSpec((None, hb, 2, chunk), lambda b, h, c: (b, h, 0, c)),
            pl.BlockSpec((None, hb, dk, dv), lambda b, h, c: (b, h, 0, 0)),
            pl.BlockSpec((None, hb, 1, dk), lambda b, h, c: (b, h, 0, 0)),
            pl.BlockSpec((None, hb, 1, LANES), lambda b, h, c: (b, h, 0, 0)),
            pl.BlockSpec((hb, 1, dv), lambda b, h, c: (h, 0, 0)),
        ],
        out_specs=[
            pl.BlockSpec((None, chunk, hb * dv), lambda b, h, c: (b, c, h)),
            pl.BlockSpec((None, hb, dk, dv), lambda b, h, c: (b, h, 0, 0)),
            pl.BlockSpec((None, hb, 1, dk), lambda b, h, c: (b, h, 0, 0)),
            pl.BlockSpec((None, hb, 1, LANES), lambda b, h, c: (b, h, 0, 0)),
        ],
        out_shape=[
            jax.ShapeDtypeStruct((bsz, t, nh * dv), BF16),
            jax.ShapeDtypeStruct((bsz, nh, dk, dv), F32),
            jax.ShapeDtypeStruct((bsz, nh, 1, dk), F32),
            jax.ShapeDtypeStruct((bsz, nh, 1, LANES), F32),
        ],
        compiler_params=_cparams("parallel", "parallel", "arbitrary"),
    )(qk, qk, v, g_col, g_row, c0, n0.reshape(bsz, nh, 1, dk), m0b, g_head.reshape(nh, 1, dv))
    return hn, c, n.reshape(bsz, nh, dk), m[:, :, 0, 0]


def _down_body(hn_ref, xc_ref, z_ref, skip_ref, w_ref, r_ref, o_ref):
    z = z_ref[...].astype(F32)
    a = (hn_ref[...].astype(F32) + skip_ref[...] * xc_ref[...].astype(F32)) * _silu(z)
    o_ref[...] = r_ref[...] + _dot(a.astype(BF16), w_ref[...])


def gated_down(hn, xc, up, skip, w, res):
    n, c = hn.shape
    d = w.shape[1]
    tm = _tile(n, 512, SUBLANES)
    return pl.pallas_call(
        _down_body,
        grid=(n // tm,),
        in_specs=[
            pl.BlockSpec((tm, c), lambda i: (i, 0)),
            pl.BlockSpec((tm, c), lambda i: (i, 0)),
            pl.BlockSpec((tm, c), lambda i: (i, 1)),
            pl.BlockSpec((1, c), lambda i: (0, 0)),
            pl.BlockSpec((c, d), lambda i: (0, 0)),
            pl.BlockSpec((tm, d), lambda i: (i, 0)),
        ],
        out_specs=pl.BlockSpec((tm, d), lambda i: (i, 0)),
        out_shape=jax.ShapeDtypeStruct((n, d), F32),
        compiler_params=_cparams("parallel"),
    )(hn, xc, up, skip.reshape(1, c), w, res)


def _ffn_body(x_ref, g_ref, wg_ref, wu_ref, wd_ref, o_ref, xn_ref):
    @pl.when(pl.program_id(1) == 0)
    def _():
        x = x_ref[...]
        xn_ref[...] = _rms(x, g_ref[...]).astype(BF16)
        o_ref[...] = x

    xn = xn_ref[...]
    hmid = _silu(_dot(xn, wg_ref[...])) * _dot(xn, wu_ref[...])
    o_ref[...] += _dot(hmid.astype(BF16), wd_ref[...])


def ffn_dense(x, g, wg, wu, wd):
    n, d = x.shape
    f = wg.shape[1]
    tm = _tile(n, 512, SUBLANES)
    tf = _tile(f, 1408, LANES)
    return pl.pallas_call(
        _ffn_body,
        grid=(n // tm, f // tf),
        in_specs=[
            pl.BlockSpec((tm, d), lambda i, j: (i, 0)),
            pl.BlockSpec((1, d), lambda i, j: (0, 0)),
            pl.BlockSpec((d, tf), lambda i, j: (0, j)),
            pl.BlockSpec((d, tf), lambda i, j: (0, j)),
            pl.BlockSpec((tf, d), lambda i, j: (j, 0)),
        ],
        out_specs=pl.BlockSpec((tm, d), lambda i, j: (i, 0)),
        out_shape=jax.ShapeDtypeStruct((n, d), F32),
        scratch_shapes=[pltpu.VMEM((tm, d), BF16)],
        compiler_params=_cparams("parallel", "arbitrary"),
    )(x, g.reshape(1, d), wg, wu, wd)


def _kv_body(x_ref, g_ref, w_ref, k32_ref, v32_ref, kb_ref, vb_ref):
    xn = _rms(x_ref[...], g_ref[...]).astype(BF16)
    kw = k32_ref.shape[1]
    k = _dot(xn, w_ref[:, :kw])
    v = _dot(xn, w_ref[:, kw:])
    k32_ref[...] = k
    v32_ref[...] = v
    kb_ref[...] = k.astype(BF16)
    vb_ref[...] = v.astype(BF16)


def shared_kv(x, g, w, kw):
    n, d = x.shape
    vw = w.shape[1] - kw
    tm = _tile(n, 512, SUBLANES)
    return pl.pallas_call(
        _kv_body,
        grid=(n // tm,),
        in_specs=[
            pl.BlockSpec((tm, d), lambda i: (i, 0)),
            pl.BlockSpec((1, d), lambda i: (0, 0)),
            pl.BlockSpec((d, kw + vw), lambda i: (0, 0)),
        ],
        out_specs=[
            pl.BlockSpec((tm, kw), lambda i: (i, 0)),
            pl.BlockSpec((tm, vw), lambda i: (i, 0)),
            pl.BlockSpec((tm, kw), lambda i: (i, 0)),
            pl.BlockSpec((tm, vw), lambda i: (i, 0)),
        ],
        out_shape=[
            jax.ShapeDtypeStruct((n, kw), F32),
            jax.ShapeDtypeStruct((n, vw), F32),
            jax.ShapeDtypeStruct((n, kw), BF16),
            jax.ShapeDtypeStruct((n, vw), BF16),
        ],
        compiler_params=_cparams("parallel"),
    )(x, g.reshape(1, d), w)


def _lambda_full(lq1_ref, lk1_ref, lq2_ref, lk2_ref, lam_init):
    a = jnp.sum(lq1_ref[...] * lk1_ref[...], axis=1, keepdims=True)
    b = jnp.sum(lq2_ref[...] * lk2_ref[...], axis=1, keepdims=True)
    return jnp.exp(a) - jnp.exp(b) + lam_init


def _attn_prefill_body(slopes_ref, q_ref, k_ref, v_ref, lq1_ref, lk1_ref, lq2_ref, lk2_ref, gs_ref,
                       o_ref, t_ref, m_ref, al_ref, l_ref, acc_ref, *, tq, dk, lam_init):
    h = pl.program_id(1)
    i = pl.program_id(2)
    nt = tq // LANES
    slope = slopes_ref[h] * LOG2E
    q = q_ref[...]
    lane = lax.broadcasted_iota(jnp.int32, q.shape, 1)
    zero = jnp.zeros_like(q)
    qz = (jnp.where(lane < dk, q, zero), jnp.where(lane >= dk, q, zero))
    r = lax.broadcasted_iota(jnp.int32, (tq, tq), 0)
    c = lax.broadcasted_iota(jnp.int32, (tq, tq), 1)
    rel = r - c
    key_bias = slope * lax.broadcasted_iota(jnp.int32, (1, tq), 1).astype(F32)
    m_ref[1] = jnp.full(m_ref.shape[1:], -jnp.inf, F32)
    l_ref[...] = jnp.zeros(l_ref.shape, F32)
    acc_ref[...] = jnp.zeros(acc_ref.shape, F32)

    def block_bias(j):
        return (-slope) * ((i - j) * tq).astype(F32)

    def scores(j, masked, src, dst):
        kb = k_ref[pl.ds(pl.multiple_of(j * tq, tq), tq), :]
        cj = block_bias(j)
        for mp in range(2):
            t = _dot_nt(qz[mp], kb) + key_bias
            if masked:
                t = jnp.where(rel >= 0, t, -jnp.inf)
            t_ref[dst, mp] = t
            tm = t[:, 0:LANES]
            for ct in range(1, nt):
                tm = jnp.maximum(tm, t[:, ct * LANES:(ct + 1) * LANES])
            m_old = m_ref[src, mp]
            m_new = jnp.maximum(m_old, jnp.max(tm, axis=1, keepdims=True) + cj)
            al_ref[dst, mp] = jnp.exp2(m_old - m_new)
            m_ref[dst, mp] = m_new

    def accumulate(j, slot):
        vb = v_ref[pl.ds(pl.multiple_of(j * tq, tq), tq), :]
        cj = block_bias(j)
        for mp in range(2):
            mb = m_ref[slot, mp] - cj
            alpha = al_ref[slot, mp]
            ps = []
            lsum = None
            for ct in range(nt):
                pc = jnp.exp2(t_ref[slot, mp, :, ct * LANES:(ct + 1) * LANES] - mb)
                lsum = pc if lsum is None else lsum + pc
                ps.append(pc.astype(BF16))
            l_ref[mp] = alpha * l_ref[mp] + lsum
            a_acc = alpha if acc_ref.shape[2] == LANES else alpha[:, 0:1]
            acc_ref[mp] = a_acc * acc_ref[mp] + _dot(jnp.concatenate(ps, axis=1), vb)

    @pl.when(i == 0)
    def _():
        scores(0, True, 1, 0)
        accumulate(0, 0)

    @pl.when(i > 0)
    def _():
        scores(0, False, 1, 0)

    n_plain = jnp.maximum(i - 1, 0)

    def pair(jj, carry):
        j = 2 * jj
        accumulate(j, 0)
        scores(j + 1, False, 0, 1)
        accumulate(j + 1, 1)
        scores(j + 2, False, 1, 0)
        return carry

    lax.fori_loop(0, n_plain // 2, pair, 0)

    @pl.when(n_plain % 2 == 1)
    def _():
        accumulate(n_plain - 1, 0)
        scores(n_plain, False, 0, 1)

    @pl.when((i > 0) & (i % 2 == 1))
    def _():
        accumulate(i - 1, 0)
        scores(i, True, 0, 1)
        accumulate(i, 1)

    @pl.when((i > 0) & (i % 2 == 0))
    def _():
        accumulate(i - 1, 1)
        scores(i, True, 1, 0)
        accumulate(i, 0)

    lam = _lambda_full(lq1_ref, lk1_ref, lq2_ref, lk2_ref, lam_init)
    l0 = jnp.sum(l_ref[0], axis=1, keepdims=True)
    l1 = jnp.sum(l_ref[1], axis=1, keepdims=True)
    o = acc_ref[0] / l0 - lam * (acc_ref[1] / l1)
    o_ref[...] = (_rms(o, gs_ref[...]) * (1.0 - lam_init)).astype(BF16)


def attn_prefill(q, k, v, slopes, lq1, lk1, lq2, lk2, g_sub, nh, lam_init):
    bsz, t, qw = q.shape
    dk2 = qw // nh
    dv = v.shape[2] // nh
    tq = _tile(t, 512, LANES)
    body = functools.partial(_attn_prefill_body, tq=tq, dk=dk2 // 2, lam_init=lam_init)
    vec = lambda a: a.reshape(1, -1)
    small = lambda n: pl.BlockSpec((1, n), lambda b, h, i: (0, 0))
    return pl.pallas_call(
        body,
        grid=(bsz, nh, t // tq),
        in_specs=[
            pl.BlockSpec(memory_space=pltpu.SMEM),
            pl.BlockSpec((None, tq, dk2), lambda b, h, i: (b, i, h)),
            pl.BlockSpec((None, t, dk2), lambda b, h, i: (b, 0, h)),
            pl.BlockSpec((None, t, dv), lambda b, h, i: (b, 0, h)),
            small(dk2 // 2), small(dk2 // 2), small(dk2 // 2), small(dk2 // 2), small(dv),
        ],
        out_specs=pl.BlockSpec((None, tq, dv), lambda b, h, i: (b, i, h)),
        out_shape=jax.ShapeDtypeStruct((bsz, t, nh * dv), BF16),
        scratch_shapes=[pltpu.VMEM((2, 2, tq, tq), F32), pltpu.VMEM((2, 2, tq, LANES), F32),
                        pltpu.VMEM((2, 2, tq, LANES), F32), pltpu.VMEM((2, tq, LANES), F32),
                        pltpu.VMEM((2, tq, dv), F32)],
        compiler_params=_cparams("parallel", "parallel", "arbitrary"),
    )(slopes, q, k, v, vec(lq1), vec(lk1), vec(lq2), vec(lk2), vec(g_sub))


def _attn_paged_body(pt_ref, q_ref, *refs, pp, nh, dk, dv, t, page, past, lam_init):
    kp_refs = refs[:pp]
    vp_refs = refs[pp:2 * pp]
    (kn_ref, vn_ref, lq1_ref, lk1_ref, lq2_ref, lk2_ref, gs_ref,
     o_ref, qbd_ref, m_ref, l_ref, acc_ref) = refs[2 * pp:]
    j = pl.program_id(1)
    nj = pl.num_programs(1)
    rows = nh * 2 * t
    hr = 2 * t
    qw = nh * 2 * dk

    @pl.when(j == 0)
    def _():
        qf = q_ref[...].astype(F32)
        qrep = jnp.concatenate([qf] * (2 * nh), axis=0)
        rr = lax.broadcasted_iota(jnp.int32, (rows, qw), 0)
        cc = lax.broadcasted_iota(jnp.int32, (rows, qw), 1)
        qbd_ref[...] = jnp.where(rr // t == cc // dk, qrep, 0.0).astype(BF16)
        m_ref[...] = jnp.full(m_ref.shape, -jnp.inf, F32)
        l_ref[...] = jnp.zeros(l_ref.shape, F32)
        acc_ref[...] = jnp.zeros(acc_ref.shape, F32)

    ri = lax.broadcasted_iota(jnp.int32, (rows, 1), 0)
    head = (ri // hr).astype(F32)
    slope = jnp.exp2(-8.0 * (head + 1.0) / nh) * LOG2E
    qpos = past + ri % t
    kl = lax.broadcasted_iota(jnp.int32, (rows, page), 1)

    def update(blocks):
        ss = []
        for s, _, kpos, valid in blocks:
            dist = qpos - kpos
            s = s - slope * dist.astype(F32)
            if valid is not None:
                s = jnp.where(valid & (dist >= 0), s, -jnp.inf)
            ss.append(s)
        s = jnp.concatenate(ss, axis=1)
        m = m_ref[...]
        m_new = jnp.maximum(m, jnp.max(s, axis=1, keepdims=True))
        p = jnp.exp2(s - m_new)
        alpha = jnp.exp2(m - m_new)
        l_ref[...] = alpha * l_ref[...] + jnp.sum(p, axis=1, keepdims=True)
        m_ref[...] = m_new
        pb = p.astype(BF16)
        for hh in range(nh):
            rs = slice(hh * hr, (hh + 1) * hr)
            vh = jnp.concatenate([blk[1](hh) for blk in blocks], axis=0)
            acc_ref[rs, :] = alpha[rs] * acc_ref[rs, :] + _dot(pb[rs], vh)

    def past_blocks():
        out = []
        for r in range(pp):
            s = _dot(qbd_ref[...], kp_refs[r][...].astype(BF16))
            v_head = lambda hh, r=r: vp_refs[r][pl.ds(hh, page, stride=nh), :].astype(BF16)
            out.append((s, v_head, (j * pp + r) * page + kl, None))
        return out

    @pl.when(j < nj - 1)
    def _():
        update(past_blocks())

    @pl.when(j == nj - 1)
    def _():
        new = (_dot_nt(qbd_ref[...], kn_ref[...]), lambda hh: vn_ref[:, hh * dv:(hh + 1) * dv], past + kl, kl < t)
        update(past_blocks() + [new])
        o = acc_ref[...] / l_ref[...]
        lam = _lambda_full(lq1_ref, lk1_ref, lq2_ref, lk2_ref, lam_init)
        for hh in range(nh):
            r0 = hh * hr
            od = o[r0:r0 + t] - lam * o[r0 + t:r0 + hr]
            o_ref[:, hh * dv:(hh + 1) * dv] = (_rms(od, gs_ref[...]) * (1.0 - lam_init)).astype(BF16)


def attn_paged(q, cache_kt, cache_v, page_table, k_new, v_new, lq1, lk1, lq2, lk2, g_sub, lam_init):
    bsz, t, qw = q.shape
    _, page, nh, dv = cache_v.shape
    vw = nh * dv
    dk = qw // (2 * nh)
    n_pages = page_table.shape[1]
    pp = 4 if n_pages % 4 == 0 else (2 if n_pages % 2 == 0 else 1)
    rows = nh * 2 * t
    kn = jnp.pad(k_new, ((0, 0), (0, page - t), (0, 0)))
    vn = jnp.pad(v_new, ((0, 0), (0, page - t), (0, 0)))
    body = functools.partial(_attn_paged_body, pp=pp, nh=nh, dk=dk, dv=dv, t=t, page=page,
                             past=n_pages * page, lam_init=lam_init)
    vec = lambda a: a.reshape(1, -1)
    small = lambda n: pl.BlockSpec((1, n), lambda b, j, pt: (0, 0))
    k_spec = lambda r: pl.BlockSpec((None, qw, page), lambda b, j, pt: (pt[b, j * pp + r], 0, 0))
    v_spec = lambda r: pl.BlockSpec((None, page * nh, dv), lambda b, j, pt: (pt[b, j * pp + r], 0, 0))
    grid_spec = pltpu.PrefetchScalarGridSpec(
        num_scalar_prefetch=1,
        grid=(bsz, n_pages // pp),
        in_specs=(
            [pl.BlockSpec((None, t, qw), lambda b, j, pt: (b, 0, 0))]
            + [k_spec(r) for r in range(pp)]
            + [v_spec(r) for r in range(pp)]
            + [pl.BlockSpec((None, page, qw), lambda b, j, pt: (b, 0, 0)),
               pl.BlockSpec((None, page, vw), lambda b, j, pt: (b, 0, 0)),
               small(dk), small(dk), small(dk), small(dk), small(dv)]
        ),
        out_specs=pl.BlockSpec((None, t, vw), lambda b, j, pt: (b, 0, 0)),
        scratch_shapes=[
            pltpu.VMEM((rows, qw), BF16),
            pltpu.VMEM((rows, 1), F32),
            pltpu.VMEM((rows, 1), F32),
            pltpu.VMEM((rows, dv), F32),
        ],
    )
    return pl.pallas_call(
        body,
        grid_spec=grid_spec,
        out_shape=jax.ShapeDtypeStruct((bsz, t, vw), BF16),
        compiler_params=_cparams("parallel", "arbitrary"),
    )(page_table, q, *([cache_kt] * pp), *([cache_v.reshape(-1, page * nh, dv)] * pp), kn, vn,
      vec(lq1), vec(lk1), vec(lq2), vec(lk2), vec(g_sub))


def _to_token_tiles(dst_ref, x, n_rows):
    nch = x.shape[1] // LANES
    for c in range(nch):
        dst_ref[pl.ds(c, n_rows, stride=nch), :] = x[:, c * LANES:(c + 1) * LANES]


def _from_token_tiles(src_ref, c, n_rows, nch):
    return src_ref[pl.ds(c, n_rows, stride=nch), :]


def _router_body(x_ref, g_ref, wr_ref, xt_ref, idx_ref, gate_ref, *, ne):
    xn = _rms(x_ref[...], g_ref[...])
    _to_token_tiles(xt_ref, xn, xn.shape[0])
    logits = jnp.dot(xn, wr_ref[...], preferred_element_type=F32, precision=lax.Precision.HIGHEST)
    lane = lax.broadcasted_iota(jnp.int32, logits.shape, 1)
    logits = jnp.where(lane < ne, logits, -jnp.inf)
    m1 = jnp.max(logits, axis=1, keepdims=True)
    i1 = jnp.min(jnp.where(logits == m1, lane, LANES), axis=1, keepdims=True)
    rest = jnp.where(lane == i1, -jnp.inf, logits)
    m2 = jnp.max(rest, axis=1, keepdims=True)
    i2 = jnp.min(jnp.where(rest == m2, lane, LANES), axis=1, keepdims=True)
    e = jnp.exp(m2 - m1)
    g1 = 1.0 / (1.0 + e)
    g2 = e / (1.0 + e)
    idx_ref[...] = jnp.where(lane == 0, i1, jnp.where(lane == 1, i2, 0))
    gate_ref[...] = jnp.where(lane == 0, g1, jnp.where(lane == 1, g2, 0.0))


def moe_router(x, g, w_router):
    n, d = x.shape
    ne = w_router.shape[1]
    wr = jnp.zeros((d, LANES), F32).at[:, :ne].set(w_router)
    tm = _tile(n, 512, SUBLANES)
    return pl.pallas_call(
        functools.partial(_router_body, ne=ne),
        grid=(n // tm,),
        in_specs=[
            pl.BlockSpec((tm, d), lambda i: (i, 0)),
            pl.BlockSpec((1, d), lambda i: (0, 0)),
            pl.BlockSpec((d, LANES), lambda i: (0, 0)),
        ],
        out_specs=[
            pl.BlockSpec((tm * (d // LANES), LANES), lambda i: (i, 0)),
            pl.BlockSpec((tm, LANES), lambda i: (i, 0)),
            pl.BlockSpec((tm, LANES), lambda i: (i, 0)),
        ],
        out_shape=[
            jax.ShapeDtypeStruct((n * (d // LANES), LANES), F32),
            jax.ShapeDtypeStruct((n, LANES), jnp.int32),
            jax.ShapeDtypeStruct((n, LANES), F32),
        ],
        compiler_params=_cparams("parallel"),
    )(x, g.reshape(1, d), wr)


DMA_ISSUE_UNROLL = 8


def _dispatch_body(pos_ref, x_hbm, xs_init_hbm, xs_hbm, sem, *, tc, nch):
    del xs_init_hbm
    i = pl.program_id(0)
    base = i * tc

    def issue(r, carry):
        src = x_hbm.at[pl.ds(pl.multiple_of((base + r) * nch, nch), nch)]
        for kk in range(TOP_K):
            p = pos_ref[TOP_K * (base + r) + kk]
            pltpu.make_async_copy(src, xs_hbm.at[pl.ds(pl.multiple_of(p * nch, nch), nch)], sem).start()
        return carry

    lax.fori_loop(0, tc, issue, 0, unroll=DMA_ISSUE_UNROLL)

    def wait_one_step():
        for kk in range(TOP_K):
            pltpu.make_async_copy(x_hbm.at[pl.ds(0, tc * nch)], xs_hbm.at[pl.ds(0, tc * nch)], sem).wait()

    @pl.when(i > 0)
    def _():
        wait_one_step()

    @pl.when(i == pl.num_programs(0) - 1)
    def _():
        wait_one_step()


def dispatch_rows(xt, pos, n_rows, nch):
    n = xt.shape[0] // nch
    tc = _tile(n, 256, SUBLANES)
    grid_spec = pltpu.PrefetchScalarGridSpec(
        num_scalar_prefetch=1,
        grid=(n // tc,),
        in_specs=[pl.BlockSpec(memory_space=pl.ANY), pl.BlockSpec(memory_space=pl.ANY)],
        out_specs=pl.BlockSpec(memory_space=pl.ANY),
        scratch_shapes=[pltpu.SemaphoreType.DMA(())],
    )
    return pl.pallas_call(
        functools.partial(_dispatch_body, tc=tc, nch=nch),
        grid_spec=grid_spec,
        out_shape=jax.ShapeDtypeStruct((n_rows * nch, LANES), xt.dtype),
        input_output_aliases={2: 0},
        compiler_params=pltpu.CompilerParams(dimension_semantics=("arbitrary",),
                                             vmem_limit_bytes=VMEM_LIMIT_BYTES),
    )(pos, xt, jnp.zeros((n_rows * nch, LANES), xt.dtype))


def _moe_ffn_body(te_ref, tv_ref, xs_ref, wg_ref, wu_ref, wd_ref, o_ref, xb_ref, acc_ref):
    i = pl.program_id(0)
    j = pl.program_id(1)
    tm, d = acc_ref.shape
    nch = d // LANES

    @pl.when(tv_ref[i] == 0)
    def _():
        @pl.when(j == 0)
        def _():
            o_ref[...] = jnp.zeros(o_ref.shape, F32)

    @pl.when(tv_ref[i] > 0)
    def _():
        @pl.when(j == 0)
        def _():
            for c in range(nch):
                xb_ref[:, c * LANES:(c + 1) * LANES] = _from_token_tiles(xs_ref, c, tm, nch).astype(BF16)

        xb = xb_ref[...]
        hmid = _silu(_dot(xb, wg_ref[...])) * _dot(xb, wu_ref[...])
        y = _dot(hmid.astype(BF16), wd_ref[...])

        @pl.when(j == 0)
        def _():
            acc_ref[...] = y

        @pl.when(j > 0)
        def _():
            acc_ref[...] += y

        @pl.when(j == pl.num_programs(1) - 1)
        def _():
            _to_token_tiles(o_ref, acc_ref[...], tm)


def moe_ffn(xs, tile_expert, tile_valid, wg, wu, wd, tm):
    d = wg.shape[1]
    nch = d // LANES
    r = xs.shape[0] // nch
    f = wg.shape[2]
    tf = _tile(f, 1408, LANES)
    grid_spec = pltpu.PrefetchScalarGridSpec(
        num_scalar_prefetch=2,
        grid=(r // tm, f // tf),
        in_specs=[
            pl.BlockSpec((tm * nch, LANES), lambda i, j, te, tv: (i, 0)),
            pl.BlockSpec((None, d, tf), lambda i, j, te, tv: (te[i], 0, j)),
            pl.BlockSpec((None, d, tf), lambda i, j, te, tv: (te[i], 0, j)),
            pl.BlockSpec((None, tf, d), lambda i, j, te, tv: (te[i], j, 0)),
        ],
        out_specs=pl.BlockSpec((tm * nch, LANES), lambda i, j, te, tv: (i, 0)),
        scratch_shapes=[pltpu.VMEM((tm, d), BF16), pltpu.VMEM((tm, d), F32)],
    )
    return pl.pallas_call(
        _moe_ffn_body,
        grid_spec=grid_spec,
        out_shape=jax.ShapeDtypeStruct((r * nch, LANES), F32),
        compiler_params=_cparams("parallel", "arbitrary"),
    )(tile_expert, tile_valid, xs, wg, wu, wd)


def _combine_body(pos_ref, x_ref, gate_ref, gf_ref, ys_hbm, o_ref, buf_ref, sem, *, tc, nch):
    i = pl.program_id(0)
    n_steps = pl.num_programs(0)

    def fetch(step, slot):
        base = step * tc

        def issue(r, carry):
            for kk in range(TOP_K):
                p = pos_ref[TOP_K * (base + r) + kk]
                pltpu.make_async_copy(ys_hbm.at[pl.ds(pl.multiple_of(p * nch, nch), nch)],
                                      buf_ref.at[slot, kk, pl.ds(pl.multiple_of(r * nch, nch), nch)],
                                      sem.at[slot]).start()
            return carry

        lax.fori_loop(0, tc, issue, 0, unroll=DMA_ISSUE_UNROLL)

    slot = i % 2

    @pl.when(i == 0)
    def _():
        fetch(0, 0)

    @pl.when(i + 1 < n_steps)
    def _():
        fetch(i + 1, 1 - slot)

    for kk in range(TOP_K):
        pltpu.make_async_copy(ys_hbm.at[pl.ds(0, tc * nch)], buf_ref.at[slot, kk], sem.at[slot]).wait()
    gate = gate_ref[...]
    ys = []
    ssq = jnp.zeros((tc, 1), F32)
    for c in range(nch):
        y = x_ref[:, c * LANES:(c + 1) * LANES]
        for kk in range(TOP_K):
            y = y + gate[:, kk:kk + 1] * buf_ref[slot, kk, pl.ds(c, tc, stride=nch), :]
        ssq = ssq + jnp.sum(y * y, axis=1, keepdims=True)
        ys.append(y)
    scale = lax.rsqrt(ssq / (nch * LANES) + EPS)
    for c in range(nch):
        o_ref[:, c * LANES:(c + 1) * LANES] = ys[c] * scale * gf_ref[:, c * LANES:(c + 1) * LANES]


def moe_combine_norm(x, gates, pos, ys, g_final):
    n, d = x.shape
    nch = d // LANES
    tc = _tile(n, 256, SUBLANES)
    grid_spec = pltpu.PrefetchScalarGridSpec(
        num_scalar_prefetch=1,
        grid=(n // tc,),
        in_specs=[
            pl.BlockSpec((tc, d), lambda i, pos: (i, 0)),
            pl.BlockSpec((tc, LANES), lambda i, pos: (i, 0)),
            pl.BlockSpec((1, d), lambda i, pos: (0, 0)),
            pl.BlockSpec(memory_space=pl.ANY),
        ],
        out_specs=pl.BlockSpec((tc, d), lambda i, pos: (i, 0)),
        scratch_shapes=[pltpu.VMEM((2, TOP_K, tc * nch, LANES), F32), pltpu.SemaphoreType.DMA((2,))],
    )
    return pl.pallas_call(
        functools.partial(_combine_body, tc=tc, nch=nch),
        grid_spec=grid_spec,
        out_shape=jax.ShapeDtypeStruct((n, d), F32),
        compiler_params=pltpu.CompilerParams(dimension_semantics=("arbitrary",),
                                             vmem_limit_bytes=VMEM_LIMIT_BYTES),
    )(pos, x, gates, g_final.reshape(1, d), ys)


def _routing_tables(top_i, ne, tm):
    n = top_i.shape[0]
    e_flat = top_i.reshape(-1)
    onehot = (e_flat[:, None] == jnp.arange(ne, dtype=jnp.int32)[None, :]).astype(jnp.int32)
    csum = jnp.cumsum(onehot, axis=0)
    rank = jnp.sum(onehot * csum, axis=1) - 1
    counts = csum[-1]
    padded = ((counts + tm - 1) // tm) * tm
    pad_end = jnp.cumsum(padded)
    pad_start = pad_end - padded
    pos = jnp.sum(onehot * pad_start[None, :], axis=1) + rank
    n_tiles = (n * TOP_K) // tm + ne
    tile_row0 = jnp.arange(n_tiles, dtype=jnp.int32) * tm
    tile_valid = (tile_row0 < pad_end[-1]).astype(jnp.int32)
    last_e = jnp.max(jnp.where(counts > 0, jnp.arange(ne, dtype=jnp.int32), 0))
    tile_expert = jnp.sum((pad_end[None, :] <= tile_row0[:, None]).astype(jnp.int32), axis=1)
    tile_expert = jnp.minimum(tile_expert, last_e)
    return pos.astype(jnp.int32), tile_expert.astype(jnp.int32), tile_valid


def moe_block_final(x, g_ffn, w_router, wg, wu, wd, g_final):
    n, d = x.shape
    ne = w_router.shape[1]
    tm = _tile(n * TOP_K, 512, SUBLANES)
    xt, idx, gates = moe_router(x, g_ffn, w_router)
    pos, tile_expert, tile_valid = _routing_tables(idx[:, :TOP_K], ne, tm)
    xs = dispatch_rows(xt, pos, tile_expert.shape[0] * tm, d // LANES)
    ys = moe_ffn(xs, tile_expert, tile_valid, wg, wu, wd, tm)
    return moe_combine_norm(x, gates, pos, ys, g_final)


def _run_group(x, c0, n0, m0, conv0, past, w):
    bsz, t, d = x.shape
    n = bsz * t
    nh_ml = c0.shape[1]
    xf = x.reshape(n, d)

    up = rms_matmul(xf, w["ml_norm"], w["ml_w_up"], BF16)
    inner = up.shape[1] // 2
    xc, conv_new = conv_silu(up.reshape(bsz, t, 2 * inner), conv0, w["ml_conv_w"], w["ml_conv_b"])
    xc = xc.reshape(n, inner)
    qk = matmul(xc, w["ml_w_qk"], BF16)
    v = matmul(up, w["ml_w_v"], BF16, col_block=0)
    gates = mlstm_gates(qk, v, w["ml_w_ig"], w["ml_b_ig"], w["ml_w_fg"], w["ml_b_fg"])
    hn, c_new, n_new, m_new = mlstm_scan(qk.reshape(bsz, t, 2 * inner), v.reshape(bsz, t, inner),
                                         gates.reshape(bsz, t, LANES), c0, n0, m0, w["ml_head_norm"])
    x1 = gated_down(hn.reshape(n, inner), xc, up, w["ml_skip"], w["ml_w_down"], xf)
    x2 = ffn_dense(x1, w["ffn_norm0"], w["mlp_w_gate"], w["mlp_w_up"], w["mlp_w_down"])

    nh_da = w["da_heads"]
    kw = w["da_w_q"].shape[1]
    k32, v32, kb, vb = shared_kv(x2, w["kv_norm"], w["w_kv"], kw)
    vw = v32.shape[1]

    q = rms_matmul(x2, w["da_norm"], w["da_w_q"], BF16).reshape(bsz, t, kw)
    lam_init = w["lam_init"]
    lam_args = (w["da_lq1"], w["da_lk1"], w["da_lq2"], w["da_lk2"], w["da_subln"])
    if past is None:
        o = attn_prefill(q, kb.reshape(bsz, t, kw), vb.reshape(bsz, t, vw), w["slopes"], *lam_args, nh_da, lam_init)
    else:
        cache_kt, cache_v, page_table = past
        o = attn_paged(q, cache_kt, cache_v, page_table, kb.reshape(bsz, t, kw), vb.reshape(bsz, t, vw),
                       *lam_args, lam_init)
    x3 = matmul_res(o.reshape(n, vw), w["da_w_o"], x2)
    y = moe_block_final(x3, w["ffn_norm1"], w["moe_router"], w["moe_w_gate"], w["moe_w_up"], w["moe_w_down"],
                        w["final_norm"])
    dk = kw // (2 * nh_da)
    return (y.reshape(bsz, t, d), c_new[None], n_new[None], m_new[None], conv_new[None],
            k32.reshape(bsz, t, nh_da, 2, dk), v32.reshape(bsz, t, nh_da, vw // nh_da))


def kernel(x_prompt, x_sample, state_mlstm_C, state_mlstm_n, state_mlstm_m, state_conv, cache_k, cache_v, page_table, ml_norm, ml_w_up, ml_conv_w, ml_conv_b, ml_w_q, ml_w_k, ml_w_v, ml_w_ig, ml_b_ig, ml_w_fg, ml_b_fg, ml_head_norm, ml_skip, ml_w_down, kv_norm, w_kv, da_norm, da_w_q, da_lq1, da_lk1, da_lq2, da_lk2, da_subln, da_w_o, ffn_norm, mlp_w_gate, mlp_w_up, mlp_w_down, moe_router, moe_w_gate, moe_w_up, moe_w_down, final_norm):
    assert ml_norm.shape[0] == 1 and da_norm.shape[0] == 1 and ffn_norm.shape[0] == 2, "one mLSTM layer then one attention layer"
    nh_da = cache_k.shape[2]
    dk = cache_k.shape[4]
    layer = 1
    w = dict(
        ml_norm=ml_norm[0], ml_w_up=ml_w_up[0].astype(BF16), ml_conv_w=ml_conv_w[0], ml_conv_b=ml_conv_b[0],
        ml_w_qk=jnp.concatenate([ml_w_q[0], ml_w_k[0]], axis=1).astype(BF16), ml_w_v=ml_w_v[0].astype(BF16),
        ml_w_ig=ml_w_ig[0], ml_b_ig=ml_b_ig[0], ml_w_fg=ml_w_fg[0], ml_b_fg=ml_b_fg[0],
        ml_head_norm=ml_head_norm[0], ml_skip=ml_skip[0], ml_w_down=ml_w_down[0].astype(BF16),
        kv_norm=kv_norm, w_kv=w_kv.astype(BF16), da_norm=da_norm[0],
        da_w_q=(da_w_q[0] * (float(dk) ** -0.5 * LOG2E)).astype(BF16),
        da_lq1=da_lq1[0], da_lk1=da_lk1[0], da_lq2=da_lq2[0], da_lk2=da_lk2[0], da_subln=da_subln[0],
        da_w_o=da_w_o[0].astype(BF16), ffn_norm0=ffn_norm[0], ffn_norm1=ffn_norm[1],
        mlp_w_gate=mlp_w_gate[0].astype(BF16), mlp_w_up=mlp_w_up[0].astype(BF16), mlp_w_down=mlp_w_down[0].astype(BF16),
        moe_router=moe_router[0], moe_w_gate=moe_w_gate[0].astype(BF16), moe_w_up=moe_w_up[0].astype(BF16),
        moe_w_down=moe_w_down[0].astype(BF16), final_norm=final_norm,
        da_heads=nh_da, lam_init=0.8 - 0.6 * math.exp(-0.3 * layer),
        slopes=jnp.exp2(-8.0 * jnp.arange(1, nh_da + 1, dtype=F32) / nh_da),
    )
    bp = x_prompt.shape[0]
    _, _, nh_ml, hd, _ = state_mlstm_C.shape
    kc = state_conv.shape[2]
    inner = state_conv.shape[3]
    zeros = lambda *s: jnp.zeros(s, F32)
    out_p = _run_group(x_prompt, zeros(bp, nh_ml, hd, hd), zeros(bp, nh_ml, hd), zeros(bp, nh_ml),
                       zeros(bp, kc, inner), None, w)
    pool, page = cache_k.shape[0], cache_k.shape[1]
    cache_kt = jnp.transpose(cache_k, (0, 2, 3, 4, 1)).reshape(pool, -1, page)
    past = (cache_kt, cache_v, page_table)
    out_s = _run_group(x_sample, state_mlstm_C[0], state_mlstm_n[0], state_mlstm_m[0], state_conv[0], past, w)
    y_p, p_c, p_n, p_m, p_conv, p_k, p_v = out_p
    y_s, s_c, s_n, s_m, s_conv, s_k, s_v = out_s
    return (y_p, y_s, p_c, p_n, p_m, p_conv, p_k, p_v, s_c, s_n, s_m, s_conv, s_k, s_v)
```

```python
import functools
import math

import jax
import jax.numpy as jnp
from jax import lax
from jax.experimental import pallas as pl
from jax.experimental.pallas import tpu as pltpu

F32 = jnp.float32
BF16 = jnp.bfloat16
EPS = 1e-6
TOP_K = 2
LOG2E = 1.4426950408889634
LANES = 128
SUBLANES = 8
VMEM_LIMIT_BYTES = 56 * 2**20


def _tile(dim, pref, align):
    t = (min(pref, dim) // align) * align
    while t >= align:
        if dim % t == 0:
            return t
        t -= align
    return dim


def _cparams(*sem):
    return pltpu.CompilerParams(dimension_semantics=sem, vmem_limit_bytes=VMEM_LIMIT_BYTES)


def _dot(a, b):
    return jnp.dot(a, b, preferred_element_type=F32)


def _dot_nt(a, b):
    return lax.dot_general(a, b, (((1,), (1,)), ((), ())), preferred_element_type=F32)


def _dot_tn(a, b):
    return lax.dot_general(a, b, (((0,), (0,)), ((), ())), preferred_element_type=F32)


def _rms(x, g):
    return x * lax.rsqrt(jnp.mean(x * x, axis=-1, keepdims=True) + EPS) * g


def _silu(x):
    return x / (1.0 + jnp.exp(-x))


def _rms_matmul_body(x_ref, g_ref, w_ref, o_ref, xn_ref):
    @pl.when(pl.program_id(1) == 0)
    def _():
        xn_ref[...] = _rms(x_ref[...], g_ref[...]).astype(BF16)

    o_ref[...] = _dot(xn_ref[...], w_ref[...]).astype(o_ref.dtype)


def rms_matmul(x, g, w, out_dtype):
    n, d = x.shape
    f = w.shape[1]
    tm = _tile(n, 1024, SUBLANES)
    tn = _tile(f, 1024 if f <= 1024 else 512, LANES)
    return pl.pallas_call(
        _rms_matmul_body,
        grid=(n // tm, f // tn),
        in_specs=[
            pl.BlockSpec((tm, d), lambda i, j: (i, 0)),
            pl.BlockSpec((1, d), lambda i, j: (0, 0)),
            pl.BlockSpec((d, tn), lambda i, j: (0, j)),
        ],
        out_specs=pl.BlockSpec((tm, tn), lambda i, j: (i, j)),
        out_shape=jax.ShapeDtypeStruct((n, f), out_dtype),
        scratch_shapes=[pltpu.VMEM((tm, d), BF16)],
        compiler_params=_cparams("parallel", "arbitrary"),
    )(x, g.reshape(1, d), w)


def _matmul_body(a_ref, w_ref, o_ref):
    o_ref[...] = _dot(a_ref[...], w_ref[...]).astype(o_ref.dtype)


def matmul(a, w, out_dtype, col_block=0):
    n = a.shape[0]
    k, f = w.shape
    tm = _tile(n, 1024, SUBLANES)
    tn = _tile(f, 512, LANES)
    return pl.pallas_call(
        _matmul_body,
        grid=(n // tm, f // tn),
        in_specs=[
            pl.BlockSpec((tm, k), lambda i, j: (i, col_block)),
            pl.BlockSpec((k, tn), lambda i, j: (0, j)),
        ],
        out_specs=pl.BlockSpec((tm, tn), lambda i, j: (i, j)),
        out_shape=jax.ShapeDtypeStruct((n, f), out_dtype),
        compiler_params=_cparams("parallel", "parallel"),
    )(a, w)


def _matmul_res_body(a_ref, w_ref, r_ref, o_ref):
    o_ref[...] = r_ref[...] + _dot(a_ref[...], w_ref[...])


def matmul_res(a, w, res):
    n, k = a.shape
    f = w.shape[1]
    tm = _tile(n, 512, SUBLANES)
    return pl.pallas_call(
        _matmul_res_body,
        grid=(n // tm,),
        in_specs=[
            pl.BlockSpec((tm, k), lambda i: (i, 0)),
            pl.BlockSpec((k, f), lambda i: (0, 0)),
            pl.BlockSpec((tm, f), lambda i: (i, 0)),
        ],
        out_specs=pl.BlockSpec((tm, f), lambda i: (i, 0)),
        out_shape=jax.ShapeDtypeStruct((n, f), F32),
        compiler_params=_cparams("parallel"),
    )(a, w, res)


def _conv_body(x_ref, c0_ref, w_ref, b_ref, xc_ref, cn_ref, buf_ref, *, bb, tb, kc, cc):
    t = pl.program_id(1)
    lo = SUBLANES - (kc - 1)
    c = x_ref.shape[2]
    for s in range(bb):
        @pl.when(t == 0)
        def _():
            buf_ref[s, lo:SUBLANES, :] = c0_ref[s]

        @pl.when(t > 0)
        def _():
            buf_ref[s, lo:SUBLANES, :] = buf_ref[s, tb + lo:tb + SUBLANES, :]

        buf_ref[s, SUBLANES:SUBLANES + tb, :] = x_ref[s].astype(F32)
        for c0 in range(0, c, cc):
            y = b_ref[:, c0:c0 + cc]
            for i in range(kc):
                y = y + buf_ref[s, lo + i:lo + i + tb, c0:c0 + cc] * w_ref[i:i + 1, c0:c0 + cc]
            xc_ref[s, :, c0:c0 + cc] = _silu(y).astype(BF16)
        cn_ref[s] = buf_ref[s, tb + lo:tb + SUBLANES, :]


def conv_silu(up, conv0, w, b):
    bsz, t, c2 = up.shape
    c = c2 // 2
    kc = w.shape[0]
    tb = _tile(t, 512, SUBLANES)
    bb = _tile(bsz, max(1, 64 // tb), 1) if tb == t else 1
    cc = _tile(c, 512, LANES)
    body = functools.partial(_conv_body, bb=bb, tb=tb, kc=kc, cc=cc)
    return pl.pallas_call(
        body,
        grid=(bsz // bb, t // tb),
        in_specs=[
            pl.BlockSpec((bb, tb, c), lambda i, j: (i, j, 0)),
            pl.BlockSpec((bb, kc - 1, c), lambda i, j: (i, 0, 0)),
            pl.BlockSpec((kc, c), lambda i, j: (0, 0)),
            pl.BlockSpec((1, c), lambda i, j: (0, 0)),
        ],
        out_specs=[
            pl.BlockSpec((bb, tb, c), lambda i, j: (i, j, 0)),
            pl.BlockSpec((bb, kc - 1, c), lambda i, j: (i, 0, 0)),
        ],
        out_shape=[
            jax.ShapeDtypeStruct((bsz, t, c), BF16),
            jax.ShapeDtypeStruct((bsz, kc - 1, c), F32),
        ],
        scratch_shapes=[pltpu.VMEM((bb, tb + SUBLANES, c), F32)],
        compiler_params=_cparams("parallel", "arbitrary"),
    )(up, conv0, w, b.reshape(1, c))


def _gate_body(qk_ref, v_ref, w_ref, b_ref, o_ref, *, nh):
    kq = qk_ref.shape[1]
    g = _dot(qk_ref[...], w_ref[0:kq, :]) + _dot(v_ref[...], w_ref[kq:, :]) + b_ref[...]
    lane = lax.broadcasted_iota(jnp.int32, g.shape, 1)
    ls = jnp.minimum(g, 0.0) - jnp.log(1.0 + jnp.exp(-jnp.abs(g)))
    o_ref[...] = jnp.where(lane >= nh, ls, g)


def mlstm_gates(qk, v, w_ig, b_ig, w_fg, b_fg):
    n = qk.shape[0]
    nh = w_ig.shape[1]
    kin = w_ig.shape[0]
    w = jnp.zeros((kin, LANES), F32).at[:, :nh].set(w_ig).at[:, nh:2 * nh].set(w_fg).astype(BF16)
    b = jnp.zeros((1, LANES), F32).at[0, :nh].set(b_ig).at[0, nh:2 * nh].set(b_fg)
    tm = _tile(n, 512, SUBLANES)
    return pl.pallas_call(
        functools.partial(_gate_body, nh=nh),
        grid=(n // tm,),
        in_specs=[
            pl.BlockSpec((tm, qk.shape[1]), lambda i: (i, 0)),
            pl.BlockSpec((tm, v.shape[1]), lambda i: (i, 0)),
            pl.BlockSpec((kin, LANES), lambda i: (0, 0)),
            pl.BlockSpec((1, LANES), lambda i: (0, 0)),
        ],
        out_specs=pl.BlockSpec((tm, LANES), lambda i: (i, 0)),
        out_shape=jax.ShapeDtypeStruct((n, LANES), F32),
        compiler_params=_cparams("parallel"),
    )(qk, v, w, b)


def _scan_body(q_ref, k_ref, v_ref, gc_ref, gr_ref, c0_ref, n0_ref, m0_ref, gh_ref,
               hn_ref, c_ref, n_ref, m_ref, *, chunk, scale, hb, dk, dv):
    @pl.when(pl.program_id(2) == 0)
    def _():
        c_ref[...] = c0_ref[...]
        n_ref[...] = n0_ref[...]
        m_ref[...] = m0_ref[...]

    row = lax.broadcasted_iota(jnp.int32, (chunk, chunk), 0)
    col = lax.broadcasted_iota(jnp.int32, (chunk, chunk), 1)
    tril = col <= row
    for hh in range(hb):
        q = q_ref[:, hh * dk:(hh + 1) * dk]
        k = k_ref[:, hh * dk:(hh + 1) * dk]
        v = v_ref[:, hh * dv:(hh + 1) * dv]
        ig_c = gc_ref[hh, :, 0:1]
        lf_c = gc_ref[hh, :, 1:2]
        ig_r = gr_ref[hh, 0:1, :]
        lf_r = gr_ref[hh, 1:2, :]
        b_c = jnp.sum(jnp.where(tril, lf_r, 0.0), axis=1, keepdims=True)
        b_r = jnp.sum(jnp.where(row <= col, lf_c, 0.0), axis=0, keepdims=True)
        m_prev = m_ref[hh, 0:1, 0:1]
        d_log = jnp.where(tril, b_c - b_r + ig_r, -jnp.inf)
        inter = b_c + m_prev
        m_t = jnp.maximum(inter, jnp.max(d_log, axis=1, keepdims=True))
        dw = jnp.exp(d_log - m_t)
        w_inter = jnp.exp(inter - m_t)
        s = _dot_nt(q, k) * (dw * scale)
        num = w_inter * _dot(q, c_ref[hh].astype(BF16)) + _dot(s.astype(BF16), v)
        qn = jnp.sum(q.astype(F32) * n_ref[hh], axis=1, keepdims=True)
        den = w_inter * qn + jnp.sum(s, axis=1, keepdims=True)
        h = num / jnp.maximum(jnp.abs(den), jnp.exp(-m_t))
        hn_ref[:, hh * dv:(hh + 1) * dv] = _rms(h, gh_ref[hh]).astype(BF16)

        b_last = b_c[chunk - 1:chunk, :]
        m_last = m_t[chunk - 1:chunk, :]
        w_last = jnp.exp(b_last - b_c + ig_c - m_last)
        scale0 = jnp.exp(b_last + m_prev - m_last)
        kw = k.astype(F32) * (w_last * scale)
        c_ref[hh] = scale0 * c_ref[hh] + _dot_tn(kw.astype(BF16), v)
        n_ref[hh] = scale0 * n_ref[hh] + jnp.sum(kw, axis=0, keepdims=True)
        m_ref[hh] = jnp.broadcast_to(m_last, (1, LANES))


def mlstm_scan(qk, v, gates, c0, n0, m0, g_head):
    bsz, t, _ = v.shape
    _, nh, dk, dv = c0.shape
    chunk = _tile(t, 256, LANES) if t % LANES == 0 else t
    gi = gates[:, :, :nh]
    gf = gates[:, :, nh:2 * nh]
    g_col = jnp.stack([gi, gf], axis=-1).transpose(0, 2, 1, 3)
    g_row = jnp.stack([gi, gf], axis=-1).transpose(0, 2, 3, 1)
    m0b = jnp.broadcast_to(m0[:, :, None, None], (bsz, nh, 1, LANES))
    hb = nh if t == chunk else 1
    ng = nh // hb
    body = functools.partial(_scan_body, chunk=chunk, scale=float(dk) ** -0.5, hb=hb, dk=dk, dv=dv)
    hn, c, n, m = pl.pallas_call(
        body,
        grid=(bsz, ng, t // chunk),
        in_specs=[
            pl.BlockSpec((None, chunk, hb * dk), lambda b, h, c: (b, c, h)),
            pl.BlockSpec((None, chunk, hb * dk), lambda b, h, c: (b, c, ng + h)),
            pl.BlockSpec((None, chunk, hb * dv), lambda b, h, c: (b, c, h)),
            pl.BlockSpec((None, hb, chunk, 2), lambda b, h, c: (b, h, c, 0)),
            pl.BlockSpec((None, hb, 2, chunk), lambda b, h, c: (b, h, 0, c)),
            pl.BlockSpec((None, hb, dk, dv), lambda b, h, c: (b, h, 0, 0)),
            pl.BlockSpec((None, hb, 1, dk), lambda b, h, c: (b, h, 0, 0)),
            pl.BlockSpec((None, hb, 1, LANES), lambda b, h, c: (b, h, 0, 0)),
            pl.BlockSpec((hb, 1, dv), lambda b, h, c: (h, 0, 0)),
        ],
        out_specs=[
            pl.BlockSpec((None, chunk, hb * dv), lambda b, h, c: (b, c, h)),
            pl.BlockSpec((None, hb, dk, dv), lambda b, h, c: (b, h, 0, 0)),
            pl.BlockSpec((None, hb, 1, dk), lambda b, h, c: (b, h, 0, 0)),
            pl.BlockSpec((None, hb, 1, LANES), lambda b, h, c: (b, h, 0, 0)),
        ],
        out_shape=[
            jax.ShapeDtypeStruct((bsz, t, nh * dv), BF16),
            jax.ShapeDtypeStruct((bsz, nh, dk, dv), F32),
            jax.ShapeDtypeStruct((bsz, nh, 1, dk), F32),
            jax.ShapeDtypeStruct((bsz, nh, 1, LANES), F32),
        ],
        compiler_params=_cparams("parallel", "parallel", "arbitrary"),
    )(qk, qk, v, g_col, g_row, c0, n0.reshape(bsz, nh, 1, dk), m0b, g_head.reshape(nh, 1, dv))
    return hn, c, n.reshape(bsz, nh, dk), m[:, :, 0, 0]


def _down_body(hn_ref, xc_ref, z_ref, skip_ref, w_ref, r_ref, o_ref):
    z = z_ref[...].astype(F32)
    a = (hn_ref[...].astype(F32) + skip_ref[...] * xc_ref[...].astype(F32)) * _silu(z)
    o_ref[...] = r_ref[...] + _dot(a.astype(BF16), w_ref[...])


def gated_down(hn, xc, up, skip, w, res):
    n, c = hn.shape
    d = w.shape[1]
    tm = _tile(n, 512, SUBLANES)
    return pl.pallas_call(
        _down_body,
        grid=(n // tm,),
        in_specs=[
            pl.BlockSpec((tm, c), lambda i: (i, 0)),
            pl.BlockSpec((tm, c), lambda i: (i, 0)),
            pl.BlockSpec((tm, c), lambda i: (i, 1)),
            pl.BlockSpec((1, c), lambda i: (0, 0)),
            pl.BlockSpec((c, d), lambda i: (0, 0)),
            pl.BlockSpec((tm, d), lambda i: (i, 0)),
        ],
        out_specs=pl.BlockSpec((tm, d), lambda i: (i, 0)),
        out_shape=jax.ShapeDtypeStruct((n, d), F32),
        compiler_params=_cparams("parallel"),
    )(hn, xc, up, skip.reshape(1, c), w, res)


def _ffn_body(x_ref, g_ref, wg_ref, wu_ref, wd_ref, o_ref, xn_ref):
    @pl.when(pl.program_id(1) == 0)
    def _():
        x = x_ref[...]
        xn_ref[...] = _rms(x, g_ref[...]).astype(BF16)
        o_ref[...] = x

    xn = xn_ref[...]
    hmid = _silu(_dot(xn, wg_ref[...])) * _dot(xn, wu_ref[...])
    o_ref[...] += _dot(hmid.astype(BF16), wd_ref[...])


def ffn_dense(x, g, wg, wu, wd):
    n, d = x.shape
    f = wg.shape[1]
    tm = _tile(n, 512, SUBLANES)
    tf = _tile(f, 1408, LANES)
    return pl.pallas_call(
        _ffn_body,
        grid=(n // tm, f // tf),
        in_specs=[
            pl.BlockSpec((tm, d), lambda i, j: (i, 0)),
            pl.BlockSpec((1, d), lambda i, j: (0, 0)),
            pl.BlockSpec((d, tf), lambda i, j: (0, j)),
            pl.BlockSpec((d, tf), lambda i, j: (0, j)),
            pl.BlockSpec((tf, d), lambda i, j: (j, 0)),
        ],
        out_specs=pl.BlockSpec((tm, d), lambda i, j: (i, 0)),
        out_shape=jax.ShapeDtypeStruct((n, d), F32),
        scratch_shapes=[pltpu.VMEM((tm, d), BF16)],
        compiler_params=_cparams("parallel", "arbitrary"),
    )(x, g.reshape(1, d), wg, wu, wd)


def _kv_body(x_ref, g_ref, w_ref, k32_ref, v32_ref, kb_ref, vb_ref):
    xn = _rms(x_ref[...], g_ref[...]).astype(BF16)
    kw = k32_ref.shape[1]
    k = _dot(xn, w_ref[:, :kw])
    v = _dot(xn, w_ref[:, kw:])
    k32_ref[...] = k
    v32_ref[...] = v
    kb_ref[...] = k.astype(BF16)
    vb_ref[...] = v.astype(BF16)


def shared_kv(x, g, w, kw):
    n, d = x.shape
    vw = w.shape[1] - kw
    tm = _tile(n, 512, SUBLANES)
    return pl.pallas_call(
        _kv_body,
        grid=(n // tm,),
        in_specs=[
            pl.BlockSpec((tm, d), lambda i: (i, 0)),
            pl.BlockSpec((1, d), lambda i: (0, 0)),
            pl.BlockSpec((d, kw + vw), lambda i: (0, 0)),
        ],
        out_specs=[
            pl.BlockSpec((tm, kw), lambda i: (i, 0)),
            pl.BlockSpec((tm, vw), lambda i: (i, 0)),
            pl.BlockSpec((tm, kw), lambda i: (i, 0)),
            pl.BlockSpec((tm, vw), lambda i: (i, 0)),
        ],
        out_shape=[
            jax.ShapeDtypeStruct((n, kw), F32),
            jax.ShapeDtypeStruct((n, vw), F32),
            jax.ShapeDtypeStruct((n, kw), BF16),
            jax.ShapeDtypeStruct((n, vw), BF16),
        ],
        compiler_params=_cparams("parallel"),
    )(x, g.reshape(1, d), w)


def _lambda_full(lq1_ref, lk1_ref, lq2_ref, lk2_ref, lam_init):
    a = jnp.sum(lq1_ref[...] * lk1_ref[...], axis=1, keepdims=True)
    b = jnp.sum(lq2_ref[...] * lk2_ref[...], axis=1, keepdims=True)
    return jnp.exp(a) - jnp.exp(b) + lam_init


def _attn_prefill_body(slopes_ref, q_ref, k_ref, v_ref, lq1_ref, lk1_ref, lq2_ref, lk2_ref, gs_ref,
                       o_ref, t_ref, m_ref, al_ref, l_ref, acc_ref, *, tq, dk, lam_init):
    h = pl.program_id(1)
    i = pl.program_id(2)
    nt = tq // LANES
    slope = slopes_ref[h] * LOG2E
    q = q_ref[...]
    lane = lax.broadcasted_iota(jnp.int32, q.shape, 1)
    zero = jnp.zeros_like(q)
    qz = (jnp.where(lane < dk, q, zero), jnp.where(lane >= dk, q, zero))
    r = lax.broadcasted_iota(jnp.int32, (tq, tq), 0)
    c = lax.broadcasted_iota(jnp.int32, (tq, tq), 1)
    rel = r - c
    key_bias = slope * lax.broadcasted_iota(jnp.int32, (1, tq), 1).astype(F32)
    m_ref[1] = jnp.full(m_ref.shape[1:], -jnp.inf, F32)
    l_ref[...] = jnp.zeros(l_ref.shape, F32)
    acc_ref[...] = jnp.zeros(acc_ref.shape, F32)

    def block_bias(j):
        return (-slope) * ((i - j) * tq).astype(F32)

    def scores(j, masked, src, dst):
        kb = k_ref[pl.ds(pl.multiple_of(j * tq, tq), tq), :]
        cj = block_bias(j)
        for mp in range(2):
            t = _dot_nt(qz[mp], kb) + key_bias
            if masked:
                t = jnp.where(rel >= 0, t, -jnp.inf)
            t_ref[dst, mp] = t
            tm = t[:, 0:LANES]
            for ct in range(1, nt):
                tm = jnp.maximum(tm, t[:, ct * LANES:(ct + 1) * LANES])
            m_old = m_ref[src, mp]
            m_new = jnp.maximum(m_old, jnp.max(tm, axis=1, keepdims=True) + cj)
            al_ref[dst, mp] = jnp.exp2(m_old - m_new)
            m_ref[dst, mp] = m_new

    def accumulate(j, slot):
        vb = v_ref[pl.ds(pl.multiple_of(j * tq, tq), tq), :]
        cj = block_bias(j)
        for mp in range(2):
            mb = m_ref[slot, mp] - cj
            alpha = al_ref[slot, mp]
            ps = []
            lsum = None
            for ct in range(nt):
                pc = jnp.exp2((t_ref[slot, mp, :, ct * LANES:(ct + 1) * LANES] - mb).astype(BF16))
                lsum = pc if lsum is None else lsum + pc
                ps.append(pc)
            l_ref[mp] = alpha * l_ref[mp] + lsum.astype(F32)
            a_acc = alpha if acc_ref.shape[2] == LANES else alpha[:, 0:1]
            acc_ref[mp] = a_acc * acc_ref[mp] + _dot(jnp.concatenate(ps, axis=1), vb)

    @pl.when(i == 0)
    def _():
        scores(0, True, 1, 0)
        accumulate(0, 0)

    @pl.when(i > 0)
    def _():
        scores(0, False, 1, 0)

    n_plain = jnp.maximum(i - 1, 0)

    def pair(jj, carry):
        j = 2 * jj
        accumulate(j, 0)
        scores(j + 1, False, 0, 1)
        accumulate(j + 1, 1)
        scores(j + 2, False, 1, 0)
        return carry

    lax.fori_loop(0, n_plain // 2, pair, 0)

    @pl.when(n_plain % 2 == 1)
    def _():
        accumulate(n_plain - 1, 0)
        scores(n_plain, False, 0, 1)

    @pl.when((i > 0) & (i % 2 == 1))
    def _():
        accumulate(i - 1, 0)
        scores(i, True, 0, 1)
        accumulate(i, 1)

    @pl.when((i > 0) & (i % 2 == 0))
    def _():
        accumulate(i - 1, 1)
        scores(i, True, 1, 0)
        accumulate(i, 0)

    lam = _lambda_full(lq1_ref, lk1_ref, lq2_ref, lk2_ref, lam_init)
    l0 = jnp.sum(l_ref[0], axis=1, keepdims=True)
    l1 = jnp.sum(l_ref[1], axis=1, keepdims=True)
    o = acc_ref[0] / l0 - lam * (acc_ref[1] / l1)
    o_ref[...] = (_rms(o, gs_ref[...]) * (1.0 - lam_init)).astype(BF16)


def attn_prefill(q, k, v, slopes, lq1, lk1, lq2, lk2, g_sub, nh, lam_init):
    bsz, t, qw = q.shape
    dk2 = qw // nh
    dv = v.shape[2] // nh
    tq = _tile(t, 512, LANES)
    body = functools.partial(_attn_prefill_body, tq=tq, dk=dk2 // 2, lam_init=lam_init)
    vec = lambda a: a.reshape(1, -1)
    small = lambda n: pl.BlockSpec((1, n), lambda b, h, i: (0, 0))
    return pl.pallas_call(
        body,
        grid=(bsz, nh, t // tq),
        in_specs=[
            pl.BlockSpec(memory_space=pltpu.SMEM),
            pl.BlockSpec((None, tq, dk2), lambda b, h, i: (b, i, h)),
            pl.BlockSpec((None, t, dk2), lambda b, h, i: (b, 0, h)),
            pl.BlockSpec((None, t, dv), lambda b, h, i: (b, 0, h)),
            small(dk2 // 2), small(dk2 // 2), small(dk2 // 2), small(dk2 // 2), small(dv),
        ],
        out_specs=pl.BlockSpec((None, tq, dv), lambda b, h, i: (b, i, h)),
        out_shape=jax.ShapeDtypeStruct((bsz, t, nh * dv), BF16),
        scratch_shapes=[pltpu.VMEM((2, 2, tq, tq), F32), pltpu.VMEM((2, 2, tq, LANES), F32),
                        pltpu.VMEM((2, 2, tq, LANES), F32), pltpu.VMEM((2, tq, LANES), F32),
                        pltpu.VMEM((2, tq, dv), F32)],
        compiler_params=_cparams("parallel", "parallel", "arbitrary"),
    )(slopes, q, k, v, vec(lq1), vec(lk1), vec(lq2), vec(lk2), vec(g_sub))


def _attn_paged_body(pt_ref, q_ref, *refs, pp, nh, dk, dv, t, page, past, lam_init):
    kp_refs = refs[:pp]
    vp_refs = refs[pp:2 * pp]
    (kn_ref, vn_ref, lq1_ref, lk1_ref, lq2_ref, lk2_ref, gs_ref,
     o_ref, qbd_ref, m_ref, l_ref, acc_ref) = refs[2 * pp:]
    j = pl.program_id(1)
    nj = pl.num_programs(1)
    rows = nh * 2 * t
    hr = 2 * t
    qw = nh * 2 * dk

    @pl.when(j == 0)
    def _():
        qf = q_ref[...].astype(F32)
        qrep = jnp.concatenate([qf] * (2 * nh), axis=0)
        rr = lax.broadcasted_iota(jnp.int32, (rows, qw), 0)
        cc = lax.broadcasted_iota(jnp.int32, (rows, qw), 1)
        qbd_ref[...] = jnp.where(rr // t == cc // dk, qrep, 0.0).astype(BF16)
        m_ref[...] = jnp.full(m_ref.shape, -jnp.inf, F32)
        l_ref[...] = jnp.zeros(l_ref.shape, F32)
        acc_ref[...] = jnp.zeros(acc_ref.shape, F32)

    ri = lax.broadcasted_iota(jnp.int32, (rows, 1), 0)
    head = (ri // hr).astype(F32)
    slope = jnp.exp2(-8.0 * (head + 1.0) / nh) * LOG2E
    qpos = past + ri % t
    kl = lax.broadcasted_iota(jnp.int32, (rows, page), 1)

    def update(blocks):
        ss = []
        for s, _, kpos, valid in blocks:
            dist = qpos - kpos
            s = s - slope * dist.astype(F32)
            if valid is not None:
                s = jnp.where(valid & (dist >= 0), s, -jnp.inf)
            ss.append(s)
        s = jnp.concatenate(ss, axis=1)
        m = m_ref[...]
        m_new = jnp.maximum(m, jnp.max(s, axis=1, keepdims=True))
        p = jnp.exp2(s - m_new)
        alpha = jnp.exp2(m - m_new)
        l_ref[...] = alpha * l_ref[...] + jnp.sum(p, axis=1, keepdims=True)
        m_ref[...] = m_new
        pb = p.astype(BF16)
        for hh in range(nh):
            rs = slice(hh * hr, (hh + 1) * hr)
            vh = jnp.concatenate([blk[1](hh) for blk in blocks], axis=0)
            acc_ref[rs, :] = alpha[rs] * acc_ref[rs, :] + _dot(pb[rs], vh)

    def past_blocks():
        out = []
        for r in range(pp):
            s = _dot(qbd_ref[...], kp_refs[r][...].astype(BF16))
            v_head = lambda hh, r=r: vp_refs[r][pl.ds(hh, page, stride=nh), :].astype(BF16)
            out.append((s, v_head, (j * pp + r) * page + kl, None))
        return out

    @pl.when(j < nj - 1)
    def _():
        update(past_blocks())

    @pl.when(j == nj - 1)
    def _():
        new = (_dot_nt(qbd_ref[...], kn_ref[...]), lambda hh: vn_ref[:, hh * dv:(hh + 1) * dv], past + kl, kl < t)
        update(past_blocks() + [new])
        o = acc_ref[...] / l_ref[...]
        lam = _lambda_full(lq1_ref, lk1_ref, lq2_ref, lk2_ref, lam_init)
        for hh in range(nh):
            r0 = hh * hr
            od = o[r0:r0 + t] - lam * o[r0 + t:r0 + hr]
            o_ref[:, hh * dv:(hh + 1) * dv] = (_rms(od, gs_ref[...]) * (1.0 - lam_init)).astype(BF16)


def attn_paged(q, cache_kt, cache_v, page_table, k_new, v_new, lq1, lk1, lq2, lk2, g_sub, lam_init):
    bsz, t, qw = q.shape
    _, page, nh, dv = cache_v.shape
    vw = nh * dv
    dk = qw // (2 * nh)
    n_pages = page_table.shape[1]
    pp = 4 if n_pages % 4 == 0 else (2 if n_pages % 2 == 0 else 1)
    rows = nh * 2 * t
    kn = jnp.pad(k_new, ((0, 0), (0, page - t), (0, 0)))
    vn = jnp.pad(v_new, ((0, 0), (0, page - t), (0, 0)))
    body = functools.partial(_attn_paged_body, pp=pp, nh=nh, dk=dk, dv=dv, t=t, page=page,
                             past=n_pages * page, lam_init=lam_init)
    vec = lambda a: a.reshape(1, -1)
    small = lambda n: pl.BlockSpec((1, n), lambda b, j, pt: (0, 0))
    k_spec = lambda r: pl.BlockSpec((None, qw, page), lambda b, j, pt: (pt[b, j * pp + r], 0, 0))
    v_spec = lambda r: pl.BlockSpec((None, page * nh, dv), lambda b, j, pt: (pt[b, j * pp + r], 0, 0))
    grid_spec = pltpu.PrefetchScalarGridSpec(
        num_scalar_prefetch=1,
        grid=(bsz, n_pages // pp),
        in_specs=(
            [pl.BlockSpec((None, t, qw), lambda b, j, pt: (b, 0, 0))]
            + [k_spec(r) for r in range(pp)]
            + [v_spec(r) for r in range(pp)]
            + [pl.BlockSpec((None, page, qw), lambda b, j, pt: (b, 0, 0)),
               pl.BlockSpec((None, page, vw), lambda b, j, pt: (b, 0, 0)),
               small(dk), small(dk), small(dk), small(dk), small(dv)]
        ),
        out_specs=pl.BlockSpec((None, t, vw), lambda b, j, pt: (b, 0, 0)),
        scratch_shapes=[
            pltpu.VMEM((rows, qw), BF16),
            pltpu.VMEM((rows, 1), F32),
            pltpu.VMEM((rows, 1), F32),
            pltpu.VMEM((rows, dv), F32),
        ],
    )
    return pl.pallas_call(
        body,
        grid_spec=grid_spec,
        out_shape=jax.ShapeDtypeStruct((bsz, t, vw), BF16),
        compiler_params=_cparams("parallel", "arbitrary"),
    )(page_table, q, *([cache_kt] * pp), *([cache_v.reshape(-1, page * nh, dv)] * pp), kn, vn,
      vec(lq1), vec(lk1), vec(lq2), vec(lk2), vec(g_sub))


def _to_token_tiles(dst_ref, x, n_rows):
    nch = x.shape[1] // LANES
    for c in range(nch):
        dst_ref[pl.ds(c, n_rows, stride=nch), :] = x[:, c * LANES:(c + 1) * LANES]


def _router_body(x_ref, g_ref, wr_ref, xn_ref, idx_ref, gate_ref, *, ne):
    xn = _rms(x_ref[...], g_ref[...])
    xn_ref[...] = xn
    logits = jnp.dot(xn, wr_ref[...], preferred_element_type=F32, precision=lax.Precision.HIGHEST)
    lane = lax.broadcasted_iota(jnp.int32, logits.shape, 1)
    logits = jnp.where(lane < ne, logits, -jnp.inf)
    m1 = jnp.max(logits, axis=1, keepdims=True)
    i1 = jnp.min(jnp.where(logits == m1, lane, LANES), axis=1, keepdims=True)
    rest = jnp.where(lane == i1, -jnp.inf, logits)
    m2 = jnp.max(rest, axis=1, keepdims=True)
    i2 = jnp.min(jnp.where(rest == m2, lane, LANES), axis=1, keepdims=True)
    e = jnp.exp(m2 - m1)
    g1 = 1.0 / (1.0 + e)
    g2 = e / (1.0 + e)
    idx_ref[...] = jnp.where(lane == 0, i1, jnp.where(lane == 1, i2, 0))
    gate_ref[...] = jnp.where(lane == 0, g1, jnp.where(lane == 1, g2, 0.0))


def moe_router(x, g, w_router):
    n, d = x.shape
    ne = w_router.shape[1]
    wr = jnp.zeros((d, LANES), F32).at[:, :ne].set(w_router)
    tm = _tile(n, 512, SUBLANES)
    return pl.pallas_call(
        functools.partial(_router_body, ne=ne),
        grid=(n // tm,),
        in_specs=[
            pl.BlockSpec((tm, d), lambda i: (i, 0)),
            pl.BlockSpec((1, d), lambda i: (0, 0)),
            pl.BlockSpec((d, LANES), lambda i: (0, 0)),
        ],
        out_specs=[
            pl.BlockSpec((tm, d), lambda i: (i, 0)),
            pl.BlockSpec((tm, LANES), lambda i: (i, 0)),
            pl.BlockSpec((tm, LANES), lambda i: (i, 0)),
        ],
        out_shape=[
            jax.ShapeDtypeStruct((n, d), F32),
            jax.ShapeDtypeStruct((n, LANES), jnp.int32),
            jax.ShapeDtypeStruct((n, LANES), F32),
        ],
        compiler_params=_cparams("parallel"),
    )(x, g.reshape(1, d), wr)


DMA_ISSUE_UNROLL = 8


def _dispatch_body(pos_ref, x_ref, xs_init_hbm, xs_hbm, sem, *, tc):
    del xs_init_hbm
    base = pl.program_id(0) * tc

    def issue(r, carry):
        for kk in range(TOP_K):
            p = pos_ref[TOP_K * (base + r) + kk]
            pltpu.make_async_copy(x_ref.at[pl.ds(r, 1)], xs_hbm.at[pl.ds(p, 1)], sem).start()
        return carry

    lax.fori_loop(0, tc, issue, 0, unroll=DMA_ISSUE_UNROLL)
    for kk in range(TOP_K):
        pltpu.make_async_copy(x_ref, xs_hbm.at[pl.ds(0, tc)], sem).wait()


def dispatch_rows(x, pos, n_rows):
    n, d = x.shape
    tc = _tile(n, 256, SUBLANES)
    grid_spec = pltpu.PrefetchScalarGridSpec(
        num_scalar_prefetch=1,
        grid=(n // tc,),
        in_specs=[pl.BlockSpec((tc, d), lambda i, pos: (i, 0)), pl.BlockSpec(memory_space=pl.ANY)],
        out_specs=pl.BlockSpec(memory_space=pl.ANY),
        scratch_shapes=[pltpu.SemaphoreType.DMA(())],
    )
    return pl.pallas_call(
        functools.partial(_dispatch_body, tc=tc),
        grid_spec=grid_spec,
        out_shape=jax.ShapeDtypeStruct((n_rows, d), x.dtype),
        input_output_aliases={2: 0},
        compiler_params=pltpu.CompilerParams(dimension_semantics=("arbitrary",),
                                             vmem_limit_bytes=VMEM_LIMIT_BYTES),
    )(pos, x, jnp.zeros((n_rows, d), x.dtype))


def _moe_ffn_body(te_ref, tv_ref, xs_ref, wg_ref, wu_ref, wd_ref, o_ref, xb_ref, acc_ref):
    i = pl.program_id(0)
    j = pl.program_id(1)
    tm, d = acc_ref.shape
    nch = d // LANES

    @pl.when(tv_ref[i] == 0)
    def _():
        @pl.when(j == 0)
        def _():
            o_ref[...] = jnp.zeros(o_ref.shape, F32)

    @pl.when(tv_ref[i] > 0)
    def _():
        @pl.when(j == 0)
        def _():
            xb_ref[...] = xs_ref[...].astype(BF16)

        xb = xb_ref[...]
        hmid = _silu(_dot(xb, wg_ref[...])) * _dot(xb, wu_ref[...])
        y = _dot(hmid.astype(BF16), wd_ref[...])

        @pl.when(j == 0)
        def _():
            acc_ref[...] = y

        @pl.when(j > 0)
        def _():
            acc_ref[...] += y

        @pl.when(j == pl.num_programs(1) - 1)
        def _():
            _to_token_tiles(o_ref, acc_ref[...], tm)


def moe_ffn(xs, tile_expert, tile_valid, wg, wu, wd, tm):
    r, d = xs.shape
    nch = d // LANES
    f = wg.shape[2]
    tf = _tile(f, 1408, LANES)
    grid_spec = pltpu.PrefetchScalarGridSpec(
        num_scalar_prefetch=2,
        grid=(r // tm, f // tf),
        in_specs=[
            pl.BlockSpec((tm, d), lambda i, j, te, tv: (i, 0)),
            pl.BlockSpec((None, d, tf), lambda i, j, te, tv: (te[i], 0, j)),
            pl.BlockSpec((None, d, tf), lambda i, j, te, tv: (te[i], 0, j)),
            pl.BlockSpec((None, tf, d), lambda i, j, te, tv: (te[i], j, 0)),
        ],
        out_specs=pl.BlockSpec((tm * nch, LANES), lambda i, j, te, tv: (i, 0)),
        scratch_shapes=[pltpu.VMEM((tm, d), BF16), pltpu.VMEM((tm, d), F32)],
    )
    return pl.pallas_call(
        _moe_ffn_body,
        grid_spec=grid_spec,
        out_shape=jax.ShapeDtypeStruct((r * nch, LANES), F32),
        compiler_params=_cparams("parallel", "arbitrary"),
    )(tile_expert, tile_valid, xs, wg, wu, wd)


def _combine_body(pos_ref, x_ref, gate_ref, gf_ref, ys_hbm, o_ref, buf_ref, sem, *, tc, nch):
    i = pl.program_id(0)
    n_steps = pl.num_programs(0)

    def fetch(step, slot):
        base = step * tc

        def issue(r, carry):
            for kk in range(TOP_K):
                p = pos_ref[TOP_K * (base + r) + kk]
                pltpu.make_async_copy(ys_hbm.at[pl.ds(pl.multiple_of(p * nch, nch), nch)],
                                      buf_ref.at[slot, kk, pl.ds(pl.multiple_of(r * nch, nch), nch)],
                                      sem.at[slot]).start()
            return carry

        lax.fori_loop(0, tc, issue, 0, unroll=DMA_ISSUE_UNROLL)

    slot = i % 2

    @pl.when(i == 0)
    def _():
        fetch(0, 0)

    @pl.when(i + 1 < n_steps)
    def _():
        fetch(i + 1, 1 - slot)

    for kk in range(TOP_K):
        pltpu.make_async_copy(ys_hbm.at[pl.ds(0, tc * nch)], buf_ref.at[slot, kk], sem.at[slot]).wait()
    gate = gate_ref[...]
    ys = []
    ssq = jnp.zeros((tc, 1), F32)
    for c in range(nch):
        y = x_ref[:, c * LANES:(c + 1) * LANES]
        for kk in range(TOP_K):
            y = y + gate[:, kk:kk + 1] * buf_ref[slot, kk, pl.ds(c, tc, stride=nch), :]
        ssq = ssq + jnp.sum(y * y, axis=1, keepdims=True)
        ys.append(y)
    scale = lax.rsqrt(ssq / (nch * LANES) + EPS)
    for c in range(nch):
        o_ref[:, c * LANES:(c + 1) * LANES] = ys[c] * scale * gf_ref[:, c * LANES:(c + 1) * LANES]


def moe_combine_norm(x, gates, pos, ys, g_final):
    n, d = x.shape
    nch = d // LANES
    tc = _tile(n, 256, SUBLANES)
    grid_spec = pltpu.PrefetchScalarGridSpec(
        num_scalar_prefetch=1,
        grid=(n // tc,),
        in_specs=[
            pl.BlockSpec((tc, d), lambda i, pos: (i, 0)),
            pl.BlockSpec((tc, LANES), lambda i, pos: (i, 0)),
            pl.BlockSpec((1, d), lambda i, pos: (0, 0)),
            pl.BlockSpec(memory_space=pl.ANY),
        ],
        out_specs=pl.BlockSpec((tc, d), lambda i, pos: (i, 0)),
        scratch_shapes=[pltpu.VMEM((2, TOP_K, tc * nch, LANES), F32), pltpu.SemaphoreType.DMA((2,))],
    )
    return pl.pallas_call(
        functools.partial(_combine_body, tc=tc, nch=nch),
        grid_spec=grid_spec,
        out_shape=jax.ShapeDtypeStruct((n, d), F32),
        compiler_params=pltpu.CompilerParams(dimension_semantics=("arbitrary",),
                                             vmem_limit_bytes=VMEM_LIMIT_BYTES),
    )(pos, x, gates, g_final.reshape(1, d), ys)


def _routing_tables(top_i, ne, tm):
    n = top_i.shape[0]
    e_flat = top_i.reshape(-1)
    onehot = (e_flat[:, None] == jnp.arange(ne, dtype=jnp.int32)[None, :]).astype(jnp.int32)
    csum = jnp.cumsum(onehot, axis=0)
    rank = jnp.sum(onehot * csum, axis=1) - 1
    counts = csum[-1]
    padded = ((counts + tm - 1) // tm) * tm
    pad_end = jnp.cumsum(padded)
    pad_start = pad_end - padded
    pos = jnp.sum(onehot * pad_start[None, :], axis=1) + rank
    n_tiles = (n * TOP_K) // tm + ne
    tile_row0 = jnp.arange(n_tiles, dtype=jnp.int32) * tm
    tile_valid = (tile_row0 < pad_end[-1]).astype(jnp.int32)
    last_e = jnp.max(jnp.where(counts > 0, jnp.arange(ne, dtype=jnp.int32), 0))
    tile_expert = jnp.sum((pad_end[None, :] <= tile_row0[:, None]).astype(jnp.int32), axis=1)
    tile_expert = jnp.minimum(tile_expert, last_e)
    return pos.astype(jnp.int32), tile_expert.astype(jnp.int32), tile_valid


def moe_block_final(x, g_ffn, w_router, wg, wu, wd, g_final):
    n, d = x.shape
    ne = w_router.shape[1]
    tm = _tile(n * TOP_K, 512, SUBLANES)
    xn, idx, gates = moe_router(x, g_ffn, w_router)
    pos, tile_expert, tile_valid = _routing_tables(idx[:, :TOP_K], ne, tm)
    xs = dispatch_rows(xn, pos, tile_expert.shape[0] * tm)
    ys = moe_ffn(xs, tile_expert, tile_valid, wg, wu, wd, tm)
    return moe_combine_norm(x, gates, pos, ys, g_final)


def _run_group(x, c0, n0, m0, conv0, past, w):
    bsz, t, d = x.shape
    n = bsz * t
    nh_ml = c0.shape[1]
    xf = x.reshape(n, d)

    up = rms_matmul(xf, w["ml_norm"], w["ml_w_up"], BF16)
    inner = up.shape[1] // 2
    xc, conv_new = conv_silu(up.reshape(bsz, t, 2 * inner), conv0, w["ml_conv_w"], w["ml_conv_b"])
    xc = xc.reshape(n, inner)
    qk = matmul(xc, w["ml_w_qk"], BF16)
    v = matmul(up, w["ml_w_v"], BF16, col_block=0)
    gates = mlstm_gates(qk, v, w["ml_w_ig"], w["ml_b_ig"], w["ml_w_fg"], w["ml_b_fg"])
    hn, c_new, n_new, m_new = mlstm_scan(qk.reshape(bsz, t, 2 * inner), v.reshape(bsz, t, inner),
                                         gates.reshape(bsz, t, LANES), c0, n0, m0, w["ml_head_norm"])
    x1 = gated_down(hn.reshape(n, inner), xc, up, w["ml_skip"], w["ml_w_down"], xf)
    x2 = ffn_dense(x1, w["ffn_norm0"], w["mlp_w_gate"], w["mlp_w_up"], w["mlp_w_down"])

    nh_da = w["da_heads"]
    kw = w["da_w_q"].shape[1]
    k32, v32, kb, vb = shared_kv(x2, w["kv_norm"], w["w_kv"], kw)
    vw = v32.shape[1]

    q = rms_matmul(x2, w["da_norm"], w["da_w_q"], BF16).reshape(bsz, t, kw)
    lam_init = w["lam_init"]
    lam_args = (w["da_lq1"], w["da_lk1"], w["da_lq2"], w["da_lk2"], w["da_subln"])
    if past is None:
        o = attn_prefill(q, kb.reshape(bsz, t, kw), vb.reshape(bsz, t, vw), w["slopes"], *lam_args, nh_da, lam_init)
    else:
        cache_kt, cache_v, page_table = past
        o = attn_paged(q, cache_kt, cache_v, page_table, kb.reshape(bsz, t, kw), vb.reshape(bsz, t, vw),
                       *lam_args, lam_init)
    x3 = matmul_res(o.reshape(n, vw), w["da_w_o"], x2)
    y = moe_block_final(x3, w["ffn_norm1"], w["moe_router"], w["moe_w_gate"], w["moe_w_up"], w["moe_w_down"],
                        w["final_norm"])
    dk = kw // (2 * nh_da)
    return (y.reshape(bsz, t, d), c_new[None], n_new[None], m_new[None], conv_new[None],
            k32.reshape(bsz, t, nh_da, 2, dk), v32.reshape(bsz, t, nh_da, vw // nh_da))


def kernel(x_prompt, x_sample, state_mlstm_C, state_mlstm_n, state_mlstm_m, state_conv, cache_k, cache_v, page_table, ml_norm, ml_w_up, ml_conv_w, ml_conv_b, ml_w_q, ml_w_k, ml_w_v, ml_w_ig, ml_b_ig, ml_w_fg, ml_b_fg, ml_head_norm, ml_skip, ml_w_down, kv_norm, w_kv, da_norm, da_w_q, da_lq1, da_lk1, da_lq2, da_lk2, da_subln, da_w_o, ffn_norm, mlp_w_gate, mlp_w_up, mlp_w_down, moe_router, moe_w_gate, moe_w_up, moe_w_down, final_norm):
    assert ml_norm.shape[0] == 1 and da_norm.shape[0] == 1 and ffn_norm.shape[0] == 2, "one mLSTM layer then one attention layer"
    nh_da = cache_k.shape[2]
    dk = cache_k.shape[4]
    layer = 1
    w = dict(
        ml_norm=ml_norm[0], ml_w_up=ml_w_up[0].astype(BF16), ml_conv_w=ml_conv_w[0], ml_conv_b=ml_conv_b[0],
        ml_w_qk=jnp.concatenate([ml_w_q[0], ml_w_k[0]], axis=1).astype(BF16), ml_w_v=ml_w_v[0].astype(BF16),
        ml_w_ig=ml_w_ig[0], ml_b_ig=ml_b_ig[0], ml_w_fg=ml_w_fg[0], ml_b_fg=ml_b_fg[0],
        ml_head_norm=ml_head_norm[0], ml_skip=ml_skip[0], ml_w_down=ml_w_down[0].astype(BF16),
        kv_norm=kv_norm, w_kv=w_kv.astype(BF16), da_norm=da_norm[0],
        da_w_q=(da_w_q[0] * (float(dk) ** -0.5 * LOG2E)).astype(BF16),
        da_lq1=da_lq1[0], da_lk1=da_lk1[0], da_lq2=da_lq2[0], da_lk2=da_lk2[0], da_subln=da_subln[0],
        da_w_o=da_w_o[0].astype(BF16), ffn_norm0=ffn_norm[0], ffn_norm1=ffn_norm[1],
        mlp_w_gate=mlp_w_gate[0].astype(BF16), mlp_w_up=mlp_w_up[0].astype(BF16), mlp_w_down=mlp_w_down[0].astype(BF16),
        moe_router=moe_router[0], moe_w_gate=moe_w_gate[0].astype(BF16), moe_w_up=moe_w_up[0].astype(BF16),
        moe_w_down=moe_w_down[0].astype(BF16), final_norm=final_norm,
        da_heads=nh_da, lam_init=0.8 - 0.6 * math.exp(-0.3 * layer),
        slopes=jnp.exp2(-8.0 * jnp.arange(1, nh_da + 1, dtype=F32) / nh_da),
    )
    bp = x_prompt.shape[0]
    _, _, nh_ml, hd, _ = state_mlstm_C.shape
    kc = state_conv.shape[2]
    inner = state_conv.shape[3]
    zeros = lambda *s: jnp.zeros(s, F32)
    out_p = _run_group(x_prompt, zeros(bp, nh_ml, hd, hd), zeros(bp, nh_ml, hd), zeros(bp, nh_ml),
                       zeros(bp, kc, inner), None, w)
    pool, page = cache_k.shape[0], cache_k.shape[1]
    cache_kt = jnp.transpose(cache_k, (0, 2, 3, 4, 1)).reshape(pool, -1, page)
    past = (cache_kt, cache_v, page_table)
    out_s = _run_group(x_sample, state_mlstm_C[0], state_mlstm_n[0], state_mlstm_m[0], state_conv[0], past, w)
    y_p, p_c, p_n, p_m, p_conv, p_k, p_v = out_p
    y_s, s_c, s_n, s_m, s_conv, s_k, s_v = out_s
    return (y_p, y_s, p_c, p_n, p_m, p_conv, p_k, p_v, s_c, s_n, s_m, s_conv, s_k, s_v)
```

```python
import functools
import math

import jax
import jax.numpy as jnp
from jax import lax
from jax.experimental import pallas as pl
from jax.experimental.pallas import tpu as pltpu

F32 = jnp.float32
BF16 = jnp.bfloat16
EPS = 1e-6
TOP_K = 2
LOG2E = 1.4426950408889634
LANES = 128
SUBLANES = 8
VMEM_LIMIT_BYTES = 56 * 2**20


def _tile(dim, pref, align):
    t = (min(pref, dim) // align) * align
    while t >= align:
        if dim % t == 0:
            return t
        t -= align
    return dim


def _cparams(*sem):
    return pltpu.CompilerParams(dimension_semantics=sem, vmem_limit_bytes=VMEM_LIMIT_BYTES)


def _dot(a, b):
    return jnp.dot(a, b, preferred_element_type=F32)


def _dot_nt(a, b):
    return lax.dot_general(a, b, (((1,), (1,)), ((), ())), preferred_element_type=F32)


def _dot_tn(a, b):
    return lax.dot_general(a, b, (((0,), (0,)), ((), ())), preferred_element_type=F32)


def _rms(x, g):
    return x * lax.rsqrt(jnp.mean(x * x, axis=-1, keepdims=True) + EPS) * g


def _silu(x):
    return x / (1.0 + jnp.exp(-x))


def _rms_matmul_body(x_ref, g_ref, w_ref, o_ref, xn_ref):
    @pl.when(pl.program_id(1) == 0)
    def _():
        xn_ref[...] = _rms(x_ref[...], g_ref[...]).astype(BF16)

    o_ref[...] = _dot(xn_ref[...], w_ref[...]).astype(o_ref.dtype)


def rms_matmul(x, g, w, out_dtype):
    n, d = x.shape
    f = w.shape[1]
    tm = _tile(n, 1024, SUBLANES)
    tn = _tile(f, 1024 if f <= 1024 else 512, LANES)
    return pl.pallas_call(
        _rms_matmul_body,
        grid=(n // tm, f // tn),
        in_specs=[
            pl.BlockSpec((tm, d), lambda i, j: (i, 0)),
            pl.BlockSpec((1, d), lambda i, j: (0, 0)),
            pl.BlockSpec((d, tn), lambda i, j: (0, j)),
        ],
        out_specs=pl.BlockSpec((tm, tn), lambda i, j: (i, j)),
        out_shape=jax.ShapeDtypeStruct((n, f), out_dtype),
        scratch_shapes=[pltpu.VMEM((tm, d), BF16)],
        compiler_params=_cparams("parallel", "arbitrary"),
    )(x, g.reshape(1, d), w)


def _matmul_body(a_ref, w_ref, o_ref):
    o_ref[...] = _dot(a_ref[...], w_ref[...]).astype(o_ref.dtype)


def matmul(a, w, out_dtype, col_block=0):
    n = a.shape[0]
    k, f = w.shape
    tm = _tile(n, 1024, SUBLANES)
    tn = _tile(f, 512, LANES)
    return pl.pallas_call(
        _matmul_body,
        grid=(n // tm, f // tn),
        in_specs=[
            pl.BlockSpec((tm, k), lambda i, j: (i, col_block)),
            pl.BlockSpec((k, tn), lambda i, j: (0, j)),
        ],
        out_specs=pl.BlockSpec((tm, tn), lambda i, j: (i, j)),
        out_shape=jax.ShapeDtypeStruct((n, f), out_dtype),
        compiler_params=_cparams("parallel", "parallel"),
    )(a, w)


def _matmul_res_body(a_ref, w_ref, r_ref, o_ref):
    o_ref[...] = r_ref[...] + _dot(a_ref[...], w_ref[...])


def matmul_res(a, w, res):
    n, k = a.shape
    f = w.shape[1]
    tm = _tile(n, 512, SUBLANES)
    return pl.pallas_call(
        _matmul_res_body,
        grid=(n // tm,),
        in_specs=[
            pl.BlockSpec((tm, k), lambda i: (i, 0)),
            pl.BlockSpec((k, f), lambda i: (0, 0)),
            pl.BlockSpec((tm, f), lambda i: (i, 0)),
        ],
        out_specs=pl.BlockSpec((tm, f), lambda i: (i, 0)),
        out_shape=jax.ShapeDtypeStruct((n, f), F32),
        compiler_params=_cparams("parallel"),
    )(a, w, res)


def _conv_body(x_ref, c0_ref, w_ref, b_ref, xc_ref, cn_ref, buf_ref, *, bb, tb, kc, cc):
    t = pl.program_id(1)
    lo = SUBLANES - (kc - 1)
    c = x_ref.shape[2]
    for s in range(bb):
        @pl.when(t == 0)
        def _():
            buf_ref[s, lo:SUBLANES, :] = c0_ref[s]

        @pl.when(t > 0)
        def _():
            buf_ref[s, lo:SUBLANES, :] = buf_ref[s, tb + lo:tb + SUBLANES, :]

        buf_ref[s, SUBLANES:SUBLANES + tb, :] = x_ref[s].astype(F32)
        for c0 in range(0, c, cc):
            y = b_ref[:, c0:c0 + cc]
            for i in range(kc):
                y = y + buf_ref[s, lo + i:lo + i + tb, c0:c0 + cc] * w_ref[i:i + 1, c0:c0 + cc]
            xc_ref[s, :, c0:c0 + cc] = _silu(y).astype(BF16)
        cn_ref[s] = buf_ref[s, tb + lo:tb + SUBLANES, :]


def conv_silu(up, conv0, w, b):
    bsz, t, c2 = up.shape
    c = c2 // 2
    kc = w.shape[0]
    tb = _tile(t, 512, SUBLANES)
    bb = _tile(bsz, max(1, 64 // tb), 1) if tb == t else 1
    cc = _tile(c, 512, LANES)
    body = functools.partial(_conv_body, bb=bb, tb=tb, kc=kc, cc=cc)
    return pl.pallas_call(
        body,
        grid=(bsz // bb, t // tb),
        in_specs=[
            pl.BlockSpec((bb, tb, c), lambda i, j: (i, j, 0)),
            pl.BlockSpec((bb, kc - 1, c), lambda i, j: (i, 0, 0)),
            pl.BlockSpec((kc, c), lambda i, j: (0, 0)),
            pl.BlockSpec((1, c), lambda i, j: (0, 0)),
        ],
        out_specs=[
            pl.BlockSpec((bb, tb, c), lambda i, j: (i, j, 0)),
            pl.BlockSpec((bb, kc - 1, c), lambda i, j: (i, 0, 0)),
        ],
        out_shape=[
            jax.ShapeDtypeStruct((bsz, t, c), BF16),
            jax.ShapeDtypeStruct((bsz, kc - 1, c), F32),
        ],
        scratch_shapes=[pltpu.VMEM((bb, tb + SUBLANES, c), F32)],
        compiler_params=_cparams("parallel", "arbitrary"),
    )(up, conv0, w, b.reshape(1, c))


def _gate_body(qk_ref, v_ref, w_ref, b_ref, o_ref, *, nh):
    kq = qk_ref.shape[1]
    g = _dot(qk_ref[...], w_ref[0:kq, :]) + _dot(v_ref[...], w_ref[kq:, :]) + b_ref[...]
    lane = lax.broadcasted_iota(jnp.int32, g.shape, 1)
    ls = jnp.minimum(g, 0.0) - jnp.log(1.0 + jnp.exp(-jnp.abs(g)))
    o_ref[...] = jnp.where(lane >= nh, ls, g)


def mlstm_gates(qk, v, w_ig, b_ig, w_fg, b_fg):
    n = qk.shape[0]
    nh = w_ig.shape[1]
    kin = w_ig.shape[0]
    w = jnp.zeros((kin, LANES), F32).at[:, :nh].set(w_ig).at[:, nh:2 * nh].set(w_fg).astype(BF16)
    b = jnp.zeros((1, LANES), F32).at[0, :nh].set(b_ig).at[0, nh:2 * nh].set(b_fg)
    tm = _tile(n, 512, SUBLANES)
    return pl.pallas_call(
        functools.partial(_gate_body, nh=nh),
        grid=(n // tm,),
        in_specs=[
            pl.BlockSpec((tm, qk.shape[1]), lambda i: (i, 0)),
            pl.BlockSpec((tm, v.shape[1]), lambda i: (i, 0)),
            pl.BlockSpec((kin, LANES), lambda i: (0, 0)),
            pl.BlockSpec((1, LANES), lambda i: (0, 0)),
        ],
        out_specs=pl.BlockSpec((tm, LANES), lambda i: (i, 0)),
        out_shape=jax.ShapeDtypeStruct((n, LANES), F32),
        compiler_params=_cparams("parallel"),
    )(qk, v, w, b)


def _scan_body(q_ref, k_ref, v_ref, gc_ref, gr_ref, c0_ref, n0_ref, m0_ref, gh_ref,
               hn_ref, c_ref, n_ref, m_ref, *, chunk, scale, hb, dk, dv):
    @pl.when(pl.program_id(2) == 0)
    def _():
        c_ref[...] = c0_ref[...]
        n_ref[...] = n0_ref[...]
        m_ref[...] = m0_ref[...]

    row = lax.broadcasted_iota(jnp.int32, (chunk, chunk), 0)
    col = lax.broadcasted_iota(jnp.int32, (chunk, chunk), 1)
    tril = col <= row
    for hh in range(hb):
        q = q_ref[:, hh * dk:(hh + 1) * dk]
        k = k_ref[:, hh * dk:(hh + 1) * dk]
        v = v_ref[:, hh * dv:(hh + 1) * dv]
        ig_c = gc_ref[hh, :, 0:1]
        lf_c = gc_ref[hh, :, 1:2]
        ig_r = gr_ref[hh, 0:1, :]
        lf_r = gr_ref[hh, 1:2, :]
        b_c = jnp.sum(jnp.where(tril, lf_r, 0.0), axis=1, keepdims=True)
        b_r = jnp.sum(jnp.where(row <= col, lf_c, 0.0), axis=0, keepdims=True)
        m_prev = m_ref[hh, 0:1, 0:1]
        d_log = jnp.where(tril, b_c - b_r + ig_r, -jnp.inf)
        inter = b_c + m_prev
        m_t = jnp.maximum(inter, jnp.max(d_log, axis=1, keepdims=True))
        dw = jnp.exp(d_log - m_t)
        w_inter = jnp.exp(inter - m_t)
        s = _dot_nt(q, k) * (dw * scale)
        num = w_inter * _dot(q, c_ref[hh].astype(BF16)) + _dot(s.astype(BF16), v)
        qn = jnp.sum(q.astype(F32) * n_ref[hh], axis=1, keepdims=True)
        den = w_inter * qn + jnp.sum(s, axis=1, keepdims=True)
        h = num / jnp.maximum(jnp.abs(den), jnp.exp(-m_t))
        hn_ref[:, hh * dv:(hh + 1) * dv] = _rms(h, gh_ref[hh]).astype(BF16)

        b_last = b_c[chunk - 1:chunk, :]
        m_last = m_t[chunk - 1:chunk, :]
        w_last = jnp.exp(b_last - b_c + ig_c - m_last)
        scale0 = jnp.exp(b_last + m_prev - m_last)
        kw = k.astype(F32) * (w_last * scale)
        c_ref[hh] = scale0 * c_ref[hh] + _dot_tn(kw.astype(BF16), v)
        n_ref[hh] = scale0 * n_ref[hh] + jnp.sum(kw, axis=0, keepdims=True)
        m_ref[hh] = jnp.broadcast_to(m_last, (1, LANES))


def mlstm_scan(qk, v, gates, c0, n0, m0, g_head):
    bsz, t, _ = v.shape
    _, nh, dk, dv = c0.shape
    chunk = _tile(t, 256, LANES) if t % LANES == 0 else t
    gi = gates[:, :, :nh]
    gf = gates[:, :, nh:2 * nh]
    g_col = jnp.stack([gi, gf], axis=-1).transpose(0, 2, 1, 3)
    g_row = jnp.stack([gi, gf], axis=-1).transpose(0, 2, 3, 1)
    m0b = jnp.broadcast_to(m0[:, :, None, None], (bsz, nh, 1, LANES))
    hb = nh if t == chunk else 1
    ng = nh // hb
    body = functools.partial(_scan_body, chunk=chunk, scale=float(dk) ** -0.5, hb=hb, dk=dk, dv=dv)
    hn, c, n, m = pl.pallas_call(
        body,
        grid=(bsz, ng, t // chunk),
        in_specs=[
            pl.BlockSpec((None, chunk, hb * dk), lambda b, h, c: (b, c, h)),
            pl.BlockSpec((None, chunk, hb * dk), lambda b, h, c: (b, c, ng + h)),
            pl.BlockSpec((None, chunk, hb * dv), lambda b, h, c: (b, c, h)),
            pl.BlockSpec((None, hb, chunk, 2), lambda b, h, c: (b, h, c, 0)),
            pl.BlockSpec((None, hb, 2, chunk), lambda b, h, c: (b, h, 0, c)),
            pl.BlockSpec((None, hb, dk, dv), lambda b, h, c: (b, h, 0, 0)),
            pl.BlockSpec((None, hb, 1, dk), lambda b, h, c: (b, h, 0, 0)),
            pl.BlockSpec((None, hb, 1, LANES), lambda b, h, c: (b, h, 0, 0)),
            pl.BlockSpec((hb, 1, dv), lambda b, h, c: (h, 0, 0)),
        ],
        out_specs=[
            pl.BlockSpec((None, chunk, hb * dv), lambda b, h, c: (b, c, h)),
            pl.BlockSpec((None, hb, dk, dv), lambda b, h, c: (b, h, 0, 0)),
            pl.BlockSpec((None, hb, 1, dk), lambda b, h, c: (b, h, 0, 0)),
            pl.BlockSpec((None, hb, 1, LANES), lambda b, h, c: (b, h, 0, 0)),
        ],
        out_shape=[
            jax.ShapeDtypeStruct((bsz, t, nh * dv), BF16),
            jax.ShapeDtypeStruct((bsz, nh, dk, dv), F32),
            jax.ShapeDtypeStruct((bsz, nh, 1, dk), F32),
            jax.ShapeDtypeStruct((bsz, nh, 1, LANES), F32),
        ],
        compiler_params=_cparams("parallel", "parallel", "arbitrary"),
    )(qk, qk, v, g_col, g_row, c0, n0.reshape(bsz, nh, 1, dk), m0b, g_head.reshape(nh, 1, dv))
    return hn, c, n.reshape(bsz, nh, dk), m[:, :, 0, 0]


def _down_body(hn_ref, xc_ref, z_ref, skip_ref, w_ref, r_ref, o_ref):
    z = z_ref[...].astype(F32)
    a = (hn_ref[...].astype(F32) + skip_ref[...] * xc_ref[...].astype(F32)) * _silu(z)
    o_ref[...] = r_ref[...] + _dot(a.astype(BF16), w_ref[...])


def gated_down(hn, xc, up, skip, w, res):
    n, c = hn.shape
    d = w.shape[1]
    tm = _tile(n, 512, SUBLANES)
    return pl.pallas_call(
        _down_body,
        grid=(n // tm,),
        in_specs=[
            pl.BlockSpec((tm, c), lambda i: (i, 0)),
            pl.BlockSpec((tm, c), lambda i: (i, 0)),
            pl.BlockSpec((tm, c), lambda i: (i, 1)),
            pl.BlockSpec((1, c), lambda i: (0, 0)),
            pl.BlockSpec((c, d), lambda i: (0, 0)),
            pl.BlockSpec((tm, d), lambda i: (i, 0)),
        ],
        out_specs=pl.BlockSpec((tm, d), lambda i: (i, 0)),
        out_shape=jax.ShapeDtypeStruct((n, d), F32),
        compiler_params=_cparams("parallel"),
    )(hn, xc, up, skip.reshape(1, c), w, res)


def _ffn_body(x_ref, g_ref, wg_ref, wu_ref, wd_ref, o_ref, xn_ref):
    @pl.when(pl.program_id(1) == 0)
    def _():
        x = x_ref[...]
        xn_ref[...] = _rms(x, g_ref[...]).astype(BF16)
        o_ref[...] = x

    xn = xn_ref[...]
    hmid = _silu(_dot(xn, wg_ref[...])) * _dot(xn, wu_ref[...])
    o_ref[...] += _dot(hmid.astype(BF16), wd_ref[...])


def ffn_dense(x, g, wg, wu, wd):
    n, d = x.shape
    f = wg.shape[1]
    tm = _tile(n, 512, SUBLANES)
    tf = _tile(f, 1408, LANES)
    return pl.pallas_call(
        _ffn_body,
        grid=(n // tm, f // tf),
        in_specs=[
            pl.BlockSpec((tm, d), lambda i, j: (i, 0)),
            pl.BlockSpec((1, d), lambda i, j: (0, 0)),
            pl.BlockSpec((d, tf), lambda i, j: (0, j)),
            pl.BlockSpec((d, tf), lambda i, j: (0, j)),
            pl.BlockSpec((tf, d), lambda i, j: (j, 0)),
        ],
        out_specs=pl.BlockSpec((tm, d), lambda i, j: (i, 0)),
        out_shape=jax.ShapeDtypeStruct((n, d), F32),
        scratch_shapes=[pltpu.VMEM((tm, d), BF16)],
        compiler_params=_cparams("parallel", "arbitrary"),
    )(x, g.reshape(1, d), wg, wu, wd)


def _kv_body(x_ref, g_ref, wk_ref, wv_ref, k32_ref, v32_ref, kb_ref, vb_ref, *, feature_major):
    xn = _rms(x_ref[...], g_ref[...]).astype(BF16)
    k = _dot_nt(wk_ref[...], xn) if feature_major else _dot(xn, wk_ref[...])
    v = _dot(xn, wv_ref[...])
    k32_ref[...] = k
    v32_ref[...] = v
    kb_ref[...] = k.astype(BF16)
    vb_ref[...] = v.astype(BF16)


def shared_kv(x, g, wk, wv, seq_len=None):
    n, d = x.shape
    kw = wk.shape[0] if seq_len else wk.shape[1]
    vw = wv.shape[1]
    tm = _tile(seq_len or n, 512, LANES if seq_len else SUBLANES)
    row_spec = lambda width: pl.BlockSpec((tm, width), lambda i: (i, 0))
    if seq_len:
        steps = seq_len // tm
        k_spec = pl.BlockSpec((None, kw, tm), lambda i: (i // steps, 0, i % steps))
        k_shape = (n // seq_len, kw, seq_len)
    else:
        k_spec, k_shape = row_spec(kw), (n, kw)
    return pl.pallas_call(
        functools.partial(_kv_body, feature_major=bool(seq_len)),
        grid=(n // tm,),
        in_specs=[
            row_spec(d),
            pl.BlockSpec((1, d), lambda i: (0, 0)),
            pl.BlockSpec(wk.shape, lambda i: (0, 0)),
            pl.BlockSpec(wv.shape, lambda i: (0, 0)),
        ],
        out_specs=[k_spec, row_spec(vw), k_spec, row_spec(vw)],
        out_shape=[
            jax.ShapeDtypeStruct(k_shape, F32),
            jax.ShapeDtypeStruct((n, vw), F32),
            jax.ShapeDtypeStruct(k_shape, BF16),
            jax.ShapeDtypeStruct((n, vw), BF16),
        ],
        compiler_params=_cparams("parallel"),
    )(x, g.reshape(1, d), wk, wv)


def _lambda_full(lq1_ref, lk1_ref, lq2_ref, lk2_ref, lam_init):
    a = jnp.sum(lq1_ref[...] * lk1_ref[...], axis=1, keepdims=True)
    b = jnp.sum(lq2_ref[...] * lk2_ref[...], axis=1, keepdims=True)
    return jnp.exp(a) - jnp.exp(b) + lam_init


def _attn_prefill_body(slopes_ref, q_ref, k_ref, v_ref, lq1_ref, lk1_ref, lq2_ref, lk2_ref, gs_ref,
                       o_ref, t_ref, m_ref, al_ref, l_ref, acc_ref, *, tq, dk, lam_init):
    h = pl.program_id(1)
    i = pl.program_id(2)
    nt = tq // LANES
    slope = slopes_ref[h] * LOG2E
    q = q_ref[...]
    lane = lax.broadcasted_iota(jnp.int32, q.shape, 1)
    zero = jnp.zeros_like(q)
    qz = (jnp.where(lane < dk, q, zero), jnp.where(lane >= dk, q, zero))
    r = lax.broadcasted_iota(jnp.int32, (tq, tq), 0)
    c = lax.broadcasted_iota(jnp.int32, (tq, tq), 1)
    rel = r - c
    key_bias = slope * lax.broadcasted_iota(jnp.int32, (1, tq), 1).astype(F32)
    m_ref[1] = jnp.full(m_ref.shape[1:], -jnp.inf, F32)
    l_ref[...] = jnp.zeros(l_ref.shape, F32)
    acc_ref[...] = jnp.zeros(acc_ref.shape, F32)

    def block_bias(j):
        return (-slope) * ((i - j) * tq).astype(F32)

    def scores(j, masked, src, dst):
        kb = k_ref[:, pl.ds(pl.multiple_of(j * tq, tq), tq)]
        cj = block_bias(j)
        for mp in range(2):
            t = _dot(qz[mp], kb) + key_bias
            if masked:
                t = jnp.where(rel >= 0, t, -jnp.inf)
            t_ref[dst, mp] = t
            tm = t[:, 0:LANES]
            for ct in range(1, nt):
                tm = jnp.maximum(tm, t[:, ct * LANES:(ct + 1) * LANES])
            m_old = m_ref[src, mp]
            m_new = jnp.maximum(m_old, jnp.max(tm, axis=1, keepdims=True) + cj)
            al_ref[dst, mp] = jnp.exp2(m_old - m_new)
            m_ref[dst, mp] = m_new

    def accumulate(j, slot):
        vb = v_ref[pl.ds(pl.multiple_of(j * tq, tq), tq), :]
        cj = block_bias(j)
        for mp in range(2):
            mb = m_ref[slot, mp] - cj
            alpha = al_ref[slot, mp]
            ps = []
            lsum = None
            for ct in range(nt):
                pc = jnp.exp2((t_ref[slot, mp, :, ct * LANES:(ct + 1) * LANES] - mb).astype(BF16))
                lsum = pc if lsum is None else lsum + pc
                ps.append(pc)
            l_ref[mp] = alpha * l_ref[mp] + lsum.astype(F32)
            a_acc = alpha if acc_ref.shape[2] == LANES else alpha[:, 0:1]
            acc_ref[mp] = a_acc * acc_ref[mp] + _dot(jnp.concatenate(ps, axis=1), vb)

    @pl.when(i == 0)
    def _():
        scores(0, True, 1, 0)
        accumulate(0, 0)

    @pl.when(i > 0)
    def _():
        scores(0, False, 1, 0)

    n_plain = jnp.maximum(i - 1, 0)

    def pair(jj, carry):
        j = 2 * jj
        accumulate(j, 0)
        scores(j + 1, False, 0, 1)
        accumulate(j + 1, 1)
        scores(j + 2, False, 1, 0)
        return carry

    lax.fori_loop(0, n_plain // 2, pair, 0)

    @pl.when(n_plain % 2 == 1)
    def _():
        accumulate(n_plain - 1, 0)
        scores(n_plain, False, 0, 1)

    @pl.when((i > 0) & (i % 2 == 1))
    def _():
        accumulate(i - 1, 0)
        scores(i, True, 0, 1)
        accumulate(i, 1)

    @pl.when((i > 0) & (i % 2 == 0))
    def _():
        accumulate(i - 1, 1)
        scores(i, True, 1, 0)
        accumulate(i, 0)

    lam = _lambda_full(lq1_ref, lk1_ref, lq2_ref, lk2_ref, lam_init)
    l0 = jnp.sum(l_ref[0], axis=1, keepdims=True)
    l1 = jnp.sum(l_ref[1], axis=1, keepdims=True)
    o = acc_ref[0] / l0 - lam * (acc_ref[1] / l1)
    o_ref[...] = (_rms(o, gs_ref[...]) * (1.0 - lam_init)).astype(BF16)


def attn_prefill(q, k, v, slopes, lq1, lk1, lq2, lk2, g_sub, nh, lam_init):
    bsz, t, qw = q.shape
    dk2 = qw // nh
    dv = v.shape[2] // nh
    tq = _tile(t, 512, LANES)
    body = functools.partial(_attn_prefill_body, tq=tq, dk=dk2 // 2, lam_init=lam_init)
    vec = lambda a: a.reshape(1, -1)
    small = lambda n: pl.BlockSpec((1, n), lambda b, h, i: (0, 0))
    return pl.pallas_call(
        body,
        grid=(bsz, nh, t // tq),
        in_specs=[
            pl.BlockSpec(memory_space=pltpu.SMEM),
            pl.BlockSpec((None, tq, dk2), lambda b, h, i: (b, i, h)),
            pl.BlockSpec((None, dk2, t), lambda b, h, i: (b, h, 0)),
            pl.BlockSpec((None, t, dv), lambda b, h, i: (b, 0, h)),
            small(dk2 // 2), small(dk2 // 2), small(dk2 // 2), small(dk2 // 2), small(dv),
        ],
        out_specs=pl.BlockSpec((None, tq, dv), lambda b, h, i: (b, i, h)),
        out_shape=jax.ShapeDtypeStruct((bsz, t, nh * dv), BF16),
        scratch_shapes=[pltpu.VMEM((2, 2, tq, tq), F32), pltpu.VMEM((2, 2, tq, LANES), F32),
                        pltpu.VMEM((2, 2, tq, LANES), F32), pltpu.VMEM((2, tq, LANES), F32),
                        pltpu.VMEM((2, tq, dv), F32)],
        compiler_params=_cparams("parallel", "parallel", "arbitrary"),
    )(slopes, q, k, v, vec(lq1), vec(lk1), vec(lq2), vec(lk2), vec(g_sub))


def _attn_paged_body(pt_ref, q_ref, *refs, pp, nh, dk, dv, t, page, past, lam_init):
    kp_refs = refs[:pp]
    vp_refs = refs[pp:2 * pp]
    (kn_ref, vn_ref, lq1_ref, lk1_ref, lq2_ref, lk2_ref, gs_ref,
     o_ref, qbd_ref, m_ref, l_ref, acc_ref) = refs[2 * pp:]
    j = pl.program_id(1)
    nj = pl.num_programs(1)
    rows = nh * 2 * t
    hr = 2 * t
    qw = nh * 2 * dk

    @pl.when(j == 0)
    def _():
        qf = q_ref[...].astype(F32)
        qrep = jnp.concatenate([qf] * (2 * nh), axis=0)
        rr = lax.broadcasted_iota(jnp.int32, (rows, qw), 0)
        cc = lax.broadcasted_iota(jnp.int32, (rows, qw), 1)
        qbd_ref[...] = jnp.where(rr // t == cc // dk, qrep, 0.0).astype(BF16)
        m_ref[...] = jnp.full(m_ref.shape, -jnp.inf, F32)
        l_ref[...] = jnp.zeros(l_ref.shape, F32)
        acc_ref[...] = jnp.zeros(acc_ref.shape, F32)

    ri = lax.broadcasted_iota(jnp.int32, (rows, 1), 0)
    head = (ri // hr).astype(F32)
    slope = jnp.exp2(-8.0 * (head + 1.0) / nh) * LOG2E
    qpos = past + ri % t
    kl = lax.broadcasted_iota(jnp.int32, (rows, page), 1)

    def update(blocks):
        ss = []
        for s, _, kpos, valid in blocks:
            dist = qpos - kpos
            s = s - slope * dist.astype(F32)
            if valid is not None:
                s = jnp.where(valid & (dist >= 0), s, -jnp.inf)
            ss.append(s)
        s = jnp.concatenate(ss, axis=1)
        m = m_ref[...]
        m_new = jnp.maximum(m, jnp.max(s, axis=1, keepdims=True))
        p = jnp.exp2(s - m_new)
        alpha = jnp.exp2(m - m_new)
        l_ref[...] = alpha * l_ref[...] + jnp.sum(p, axis=1, keepdims=True)
        m_ref[...] = m_new
        pb = p.astype(BF16)
        for hh in range(nh):
            rs = slice(hh * hr, (hh + 1) * hr)
            vh = jnp.concatenate([blk[1](hh) for blk in blocks], axis=0)
            acc_ref[rs, :] = alpha[rs] * acc_ref[rs, :] + _dot(pb[rs], vh)

    def past_blocks():
        out = []
        for r in range(pp):
            s = _dot(qbd_ref[...], kp_refs[r][...].astype(BF16))
            v_head = lambda hh, r=r: vp_refs[r][pl.ds(hh, page, stride=nh), :].astype(BF16)
            out.append((s, v_head, (j * pp + r) * page + kl, None))
        return out

    @pl.when(j < nj - 1)
    def _():
        update(past_blocks())

    @pl.when(j == nj - 1)
    def _():
        new = (_dot_nt(qbd_ref[...], kn_ref[...]), lambda hh: vn_ref[:, hh * dv:(hh + 1) * dv], past + kl, kl < t)
        update(past_blocks() + [new])
        o = acc_ref[...] / l_ref[...]
        lam = _lambda_full(lq1_ref, lk1_ref, lq2_ref, lk2_ref, lam_init)
        for hh in range(nh):
            r0 = hh * hr
            od = o[r0:r0 + t] - lam * o[r0 + t:r0 + hr]
            o_ref[:, hh * dv:(hh + 1) * dv] = (_rms(od, gs_ref[...]) * (1.0 - lam_init)).astype(BF16)


def attn_paged(q, cache_kt, cache_v, page_table, k_new, v_new, lq1, lk1, lq2, lk2, g_sub, lam_init):
    bsz, t, qw = q.shape
    _, page, nh, dv = cache_v.shape
    vw = nh * dv
    dk = qw // (2 * nh)
    n_pages = page_table.shape[1]
    pp = 4 if n_pages % 4 == 0 else (2 if n_pages % 2 == 0 else 1)
    rows = nh * 2 * t
    kn = jnp.pad(k_new, ((0, 0), (0, page - t), (0, 0)))
    vn = jnp.pad(v_new, ((0, 0), (0, page - t), (0, 0)))
    body = functools.partial(_attn_paged_body, pp=pp, nh=nh, dk=dk, dv=dv, t=t, page=page,
                             past=n_pages * page, lam_init=lam_init)
    vec = lambda a: a.reshape(1, -1)
    small = lambda n: pl.BlockSpec((1, n), lambda b, j, pt: (0, 0))
    k_spec = lambda r: pl.BlockSpec((None, qw, page), lambda b, j, pt: (pt[b, j * pp + r], 0, 0))
    v_spec = lambda r: pl.BlockSpec((None, page * nh, dv), lambda b, j, pt: (pt[b, j * pp + r], 0, 0))
    grid_spec = pltpu.PrefetchScalarGridSpec(
        num_scalar_prefetch=1,
        grid=(bsz, n_pages // pp),
        in_specs=(
            [pl.BlockSpec((None, t, qw), lambda b, j, pt: (b, 0, 0))]
            + [k_spec(r) for r in range(pp)]
            + [v_spec(r) for r in range(pp)]
            + [pl.BlockSpec((None, page, qw), lambda b, j, pt: (b, 0, 0)),
               pl.BlockSpec((None, page, vw), lambda b, j, pt: (b, 0, 0)),
               small(dk), small(dk), small(dk), small(dk), small(dv)]
        ),
        out_specs=pl.BlockSpec((None, t, vw), lambda b, j, pt: (b, 0, 0)),
        scratch_shapes=[
            pltpu.VMEM((rows, qw), BF16),
            pltpu.VMEM((rows, 1), F32),
            pltpu.VMEM((rows, 1), F32),
            pltpu.VMEM((rows, dv), F32),
        ],
    )
    return pl.pallas_call(
        body,
        grid_spec=grid_spec,
        out_shape=jax.ShapeDtypeStruct((bsz, t, vw), BF16),
        compiler_params=_cparams("parallel", "arbitrary"),
    )(page_table, q, *([cache_kt] * pp), *([cache_v.reshape(-1, page * nh, dv)] * pp), kn, vn,
      vec(lq1), vec(lk1), vec(lq2), vec(lk2), vec(g_sub))


def _to_token_tiles(dst_ref, x, n_rows):
    nch = x.shape[1] // LANES
    for c in range(nch):
        dst_ref[pl.ds(c, n_rows, stride=nch), :] = x[:, c * LANES:(c + 1) * LANES]


def _router_body(x_ref, g_ref, wr_ref, xn_ref, idx_ref, gate_ref, *, ne):
    xn = _rms(x_ref[...], g_ref[...])
    xn_ref[...] = xn
    logits = jnp.dot(xn, wr_ref[...], preferred_element_type=F32, precision=lax.Precision.HIGHEST)
    lane = lax.broadcasted_iota(jnp.int32, logits.shape, 1)
    logits = jnp.where(lane < ne, logits, -jnp.inf)
    m1 = jnp.max(logits, axis=1, keepdims=True)
    i1 = jnp.min(jnp.where(logits == m1, lane, LANES), axis=1, keepdims=True)
    rest = jnp.where(lane == i1, -jnp.inf, logits)
    m2 = jnp.max(rest, axis=1, keepdims=True)
    i2 = jnp.min(jnp.where(rest == m2, lane, LANES), axis=1, keepdims=True)
    e = jnp.exp(m2 - m1)
    g1 = 1.0 / (1.0 + e)
    g2 = e / (1.0 + e)
    idx_ref[...] = jnp.where(lane == 0, i1, jnp.where(lane == 1, i2, 0))
    gate_ref[...] = jnp.where(lane == 0, g1, jnp.where(lane == 1, g2, 0.0))


def moe_router(x, g, w_router):
    n, d = x.shape
    ne = w_router.shape[1]
    wr = jnp.zeros((d, LANES), F32).at[:, :ne].set(w_router)
    tm = _tile(n, 512, SUBLANES)
    return pl.pallas_call(
        functools.partial(_router_body, ne=ne),
        grid=(n // tm,),
        in_specs=[
            pl.BlockSpec((tm, d), lambda i: (i, 0)),
            pl.BlockSpec((1, d), lambda i: (0, 0)),
            pl.BlockSpec((d, LANES), lambda i: (0, 0)),
        ],
        out_specs=[
            pl.BlockSpec((tm, d), lambda i: (i, 0)),
            pl.BlockSpec((tm, LANES), lambda i: (i, 0)),
            pl.BlockSpec((tm, LANES), lambda i: (i, 0)),
        ],
        out_shape=[
            jax.ShapeDtypeStruct((n, d), F32),
            jax.ShapeDtypeStruct((n, LANES), jnp.int32),
            jax.ShapeDtypeStruct((n, LANES), F32),
        ],
        compiler_params=_cparams("parallel"),
    )(x, g.reshape(1, d), wr)


DMA_ISSUE_UNROLL = 8


def _dispatch_body(pos_ref, x_ref, xs_init_hbm, xs_hbm, sem, *, tc):
    del xs_init_hbm
    base = pl.program_id(0) * tc

    def issue(r, carry):
        for kk in range(TOP_K):
            p = pos_ref[TOP_K * (base + r) + kk]
            pltpu.make_async_copy(x_ref.at[pl.ds(r, 1)], xs_hbm.at[pl.ds(p, 1)], sem).start()
        return carry

    lax.fori_loop(0, tc, issue, 0, unroll=DMA_ISSUE_UNROLL)
    for kk in range(TOP_K):
        pltpu.make_async_copy(x_ref, xs_hbm.at[pl.ds(0, tc)], sem).wait()


def dispatch_rows(x, pos, n_rows):
    n, d = x.shape
    tc = _tile(n, 256, SUBLANES)
    grid_spec = pltpu.PrefetchScalarGridSpec(
        num_scalar_prefetch=1,
        grid=(n // tc,),
        in_specs=[pl.BlockSpec((tc, d), lambda i, pos: (i, 0)), pl.BlockSpec(memory_space=pl.ANY)],
        out_specs=pl.BlockSpec(memory_space=pl.ANY),
        scratch_shapes=[pltpu.SemaphoreType.DMA(())],
    )
    return pl.pallas_call(
        functools.partial(_dispatch_body, tc=tc),
        grid_spec=grid_spec,
        out_shape=jax.ShapeDtypeStruct((n_rows, d), x.dtype),
        input_output_aliases={2: 0},
        compiler_params=pltpu.CompilerParams(dimension_semantics=("arbitrary",),
                                             vmem_limit_bytes=VMEM_LIMIT_BYTES),
    )(pos, x, jnp.zeros((n_rows, d), x.dtype))


def _moe_ffn_body(te_ref, tv_ref, xs_ref, wg_ref, wu_ref, wd_ref, o_ref, xb_ref, acc_ref):
    i = pl.program_id(0)
    j = pl.program_id(1)
    tm, d = acc_ref.shape
    nch = d // LANES

    @pl.when(tv_ref[i] == 0)
    def _():
        @pl.when(j == 0)
        def _():
            o_ref[...] = jnp.zeros(o_ref.shape, F32)

    @pl.when(tv_ref[i] > 0)
    def _():
        @pl.when(j == 0)
        def _():
            xb_ref[...] = xs_ref[...].astype(BF16)

        xb = xb_ref[...]
        hmid = _silu(_dot(xb, wg_ref[...])) * _dot(xb, wu_ref[...])
        y = _dot(hmid.astype(BF16), wd_ref[...])

        @pl.when(j == 0)
        def _():
            acc_ref[...] = y

        @pl.when(j > 0)
        def _():
            acc_ref[...] += y

        @pl.when(j == pl.num_programs(1) - 1)
        def _():
            _to_token_tiles(o_ref, acc_ref[...], tm)


def moe_ffn(xs, tile_expert, tile_valid, wg, wu, wd, tm):
    r, d = xs.shape
    nch = d // LANES
    f = wg.shape[2]
    tf = _tile(f, 1408, LANES)
    grid_spec = pltpu.PrefetchScalarGridSpec(
        num_scalar_prefetch=2,
        grid=(r // tm, f // tf),
        in_specs=[
            pl.BlockSpec((tm, d), lambda i, j, te, tv: (i, 0)),
            pl.BlockSpec((None, d, tf), lambda i, j, te, tv: (te[i], 0, j)),
            pl.BlockSpec((None, d, tf), lambda i, j, te, tv: (te[i], 0, j)),
            pl.BlockSpec((None, tf, d), lambda i, j, te, tv: (te[i], j, 0)),
        ],
        out_specs=pl.BlockSpec((tm * nch, LANES), lambda i, j, te, tv: (i, 0)),
        scratch_shapes=[pltpu.VMEM((tm, d), BF16), pltpu.VMEM((tm, d), F32)],
    )
    return pl.pallas_call(
        _moe_ffn_body,
        grid_spec=grid_spec,
        out_shape=jax.ShapeDtypeStruct((r * nch, LANES), F32),
        compiler_params=_cparams("parallel", "arbitrary"),
    )(tile_expert, tile_valid, xs, wg, wu, wd)


def _combine_body(pos_ref, x_ref, gate_ref, gf_ref, ys_hbm, o_ref, buf_ref, sem, *, tc, nch):
    i = pl.program_id(0)
    n_steps = pl.num_programs(0)

    def fetch(step, slot):
        base = step * tc

        def issue(r, carry):
            for kk in range(TOP_K):
                p = pos_ref[TOP_K * (base + r) + kk]
                pltpu.make_async_copy(ys_hbm.at[pl.ds(pl.multiple_of(p * nch, nch), nch)],
                                      buf_ref.at[slot, kk, pl.ds(pl.multiple_of(r * nch, nch), nch)],
                                      sem.at[slot]).start()
            return carry

        lax.fori_loop(0, tc, issue, 0, unroll=DMA_ISSUE_UNROLL)

    slot = i % 2

    @pl.when(i == 0)
    def _():
        fetch(0, 0)

    @pl.when(i + 1 < n_steps)
    def _():
        fetch(i + 1, 1 - slot)

    for kk in range(TOP_K):
        pltpu.make_async_copy(ys_hbm.at[pl.ds(0, tc * nch)], buf_ref.at[slot, kk], sem.at[slot]).wait()
    gate = gate_ref[...]
    ys = []
    ssq = jnp.zeros((tc, 1), F32)
    for c in range(nch):
        y = x_ref[:, c * LANES:(c + 1) * LANES]
        for kk in range(TOP_K):
            y = y + gate[:, kk:kk + 1] * buf_ref[slot, kk, pl.ds(c, tc, stride=nch), :]
        ssq = ssq + jnp.sum(y * y, axis=1, keepdims=True)
        ys.append(y)
    scale = lax.rsqrt(ssq / (nch * LANES) + EPS)
    for c in range(nch):
        o_ref[:, c * LANES:(c + 1) * LANES] = ys[c] * scale * gf_ref[:, c * LANES:(c + 1) * LANES]


def moe_combine_norm(x, gates, pos, ys, g_final):
    n, d = x.shape
    nch = d // LANES
    tc = _tile(n, 256, SUBLANES)
    grid_spec = pltpu.PrefetchScalarGridSpec(
        num_scalar_prefetch=1,
        grid=(n // tc,),
        in_specs=[
            pl.BlockSpec((tc, d), lambda i, pos: (i, 0)),
            pl.BlockSpec((tc, LANES), lambda i, pos: (i, 0)),
            pl.BlockSpec((1, d), lambda i, pos: (0, 0)),
            pl.BlockSpec(memory_space=pl.ANY),
        ],
        out_specs=pl.BlockSpec((tc, d), lambda i, pos: (i, 0)),
        scratch_shapes=[pltpu.VMEM((2, TOP_K, tc * nch, LANES), F32), pltpu.SemaphoreType.DMA((2,))],
    )
    return pl.pallas_call(
        functools.partial(_combine_body, tc=tc, nch=nch),
        grid_spec=grid_spec,
        out_shape=jax.ShapeDtypeStruct((n, d), F32),
        compiler_params=pltpu.CompilerParams(dimension_semantics=("arbitrary",),
                                             vmem_limit_bytes=VMEM_LIMIT_BYTES),
    )(pos, x, gates, g_final.reshape(1, d), ys)


def _routing_tables(top_i, ne, tm):
    n = top_i.shape[0]
    e_flat = top_i.reshape(-1)
    onehot = (e_flat[:, None] == jnp.arange(ne, dtype=jnp.int32)[None, :]).astype(jnp.int32)
    csum = jnp.cumsum(onehot, axis=0)
    rank = jnp.sum(onehot * csum, axis=1) - 1
    counts = csum[-1]
    padded = ((counts + tm - 1) // tm) * tm
    pad_end = jnp.cumsum(padded)
    pad_start = pad_end - padded
    pos = jnp.sum(onehot * pad_start[None, :], axis=1) + rank
    n_tiles = (n * TOP_K) // tm + ne
    tile_row0 = jnp.arange(n_tiles, dtype=jnp.int32) * tm
    tile_valid = (tile_row0 < pad_end[-1]).astype(jnp.int32)
    last_e = jnp.max(jnp.where(counts > 0, jnp.arange(ne, dtype=jnp.int32), 0))
    tile_expert = jnp.sum((pad_end[None, :] <= tile_row0[:, None]).astype(jnp.int32), axis=1)
    tile_expert = jnp.minimum(tile_expert, last_e)
    return pos.astype(jnp.int32), tile_expert.astype(jnp.int32), tile_valid


def moe_block_final(x, g_ffn, w_router, wg, wu, wd, g_final):
    n, d = x.shape
    ne = w_router.shape[1]
    tm = _tile(n * TOP_K, 512, SUBLANES)
    xn, idx, gates = moe_router(x, g_ffn, w_router)
    pos, tile_expert, tile_valid = _routing_tables(idx[:, :TOP_K], ne, tm)
    xs = dispatch_rows(xn, pos, tile_expert.shape[0] * tm)
    ys = moe_ffn(xs, tile_expert, tile_valid, wg, wu, wd, tm)
    return moe_combine_norm(x, gates, pos, ys, g_final)


def _run_group(x, c0, n0, m0, conv0, past, w):
    bsz, t, d = x.shape
    n = bsz * t
    nh_ml = c0.shape[1]
    xf = x.reshape(n, d)

    up = rms_matmul(xf, w["ml_norm"], w["ml_w_up"], BF16)
    inner = up.shape[1] // 2
    xc, conv_new = conv_silu(up.reshape(bsz, t, 2 * inner), conv0, w["ml_conv_w"], w["ml_conv_b"])
    xc = xc.reshape(n, inner)
    qk = matmul(xc, w["ml_w_qk"], BF16)
    v = matmul(up, w["ml_w_v"], BF16, col_block=0)
    gates = mlstm_gates(qk, v, w["ml_w_ig"], w["ml_b_ig"], w["ml_w_fg"], w["ml_b_fg"])
    hn, c_new, n_new, m_new = mlstm_scan(qk.reshape(bsz, t, 2 * inner), v.reshape(bsz, t, inner),
                                         gates.reshape(bsz, t, LANES), c0, n0, m0, w["ml_head_norm"])
    x1 = gated_down(hn.reshape(n, inner), xc, up, w["ml_skip"], w["ml_w_down"], xf)
    x2 = ffn_dense(x1, w["ffn_norm0"], w["mlp_w_gate"], w["mlp_w_up"], w["mlp_w_down"])

    nh_da = w["da_heads"]
    kw = w["da_w_q"].shape[1]
    dk = kw // (2 * nh_da)
    long_seq = past is None and t % LANES == 0
    if long_seq:
        k32, v32, kb, vb = shared_kv(x2, w["kv_norm"], w["w_k_t"], w["w_v"], seq_len=t)
        k_leaf = jnp.transpose(k32.reshape(bsz, nh_da, 2, dk, t), (0, 4, 1, 2, 3))
    else:
        k32, v32, kb, vb = shared_kv(x2, w["kv_norm"], w["w_k"], w["w_v"])
        k_leaf = k32.reshape(bsz, t, nh_da, 2, dk)
    vw = v32.shape[1]

    q = rms_matmul(x2, w["da_norm"], w["da_w_q"], BF16).reshape(bsz, t, kw)
    lam_init = w["lam_init"]
    lam_args = (w["da_lq1"], w["da_lk1"], w["da_lq2"], w["da_lk2"], w["da_subln"])
    if past is None:
        kt = kb if long_seq else jnp.transpose(kb.reshape(bsz, t, kw), (0, 2, 1))
        o = attn_prefill(q, kt, vb.reshape(bsz, t, vw), w["slopes"], *lam_args, nh_da, lam_init)
    else:
        cache_kt, cache_v, page_table = past
        o = attn_paged(q, cache_kt, cache_v, page_table, kb.reshape(bsz, t, kw), vb.reshape(bsz, t, vw),
                       *lam_args, lam_init)
    x3 = matmul_res(o.reshape(n, vw), w["da_w_o"], x2)
    y = moe_block_final(x3, w["ffn_norm1"], w["moe_router"], w["moe_w_gate"], w["moe_w_up"], w["moe_w_down"],
                        w["final_norm"])
    return (y.reshape(bsz, t, d), c_new[None], n_new[None], m_new[None], conv_new[None],
            k_leaf, v32.reshape(bsz, t, nh_da, vw // nh_da))


def kernel(x_prompt, x_sample, state_mlstm_C, state_mlstm_n, state_mlstm_m, state_conv, cache_k, cache_v, page_table, ml_norm, ml_w_up, ml_conv_w, ml_conv_b, ml_w_q, ml_w_k, ml_w_v, ml_w_ig, ml_b_ig, ml_w_fg, ml_b_fg, ml_head_norm, ml_skip, ml_w_down, kv_norm, w_kv, da_norm, da_w_q, da_lq1, da_lk1, da_lq2, da_lk2, da_subln, da_w_o, ffn_norm, mlp_w_gate, mlp_w_up, mlp_w_down, moe_router, moe_w_gate, moe_w_up, moe_w_down, final_norm):
    assert ml_norm.shape[0] == 1 and da_norm.shape[0] == 1 and ffn_norm.shape[0] == 2, "one mLSTM layer then one attention layer"
    nh_da = cache_k.shape[2]
    dk = cache_k.shape[4]
    layer = 1
    w = dict(
        ml_norm=ml_norm[0], ml_w_up=ml_w_up[0].astype(BF16), ml_conv_w=ml_conv_w[0], ml_conv_b=ml_conv_b[0],
        ml_w_qk=jnp.concatenate([ml_w_q[0], ml_w_k[0]], axis=1).astype(BF16), ml_w_v=ml_w_v[0].astype(BF16),
        ml_w_ig=ml_w_ig[0], ml_b_ig=ml_b_ig[0], ml_w_fg=ml_w_fg[0], ml_b_fg=ml_b_fg[0],
        ml_head_norm=ml_head_norm[0], ml_skip=ml_skip[0], ml_w_down=ml_w_down[0].astype(BF16),
        kv_norm=kv_norm, w_k=w_kv[:, :da_w_q.shape[2]].astype(BF16), w_k_t=w_kv[:, :da_w_q.shape[2]].T.astype(BF16),
        w_v=w_kv[:, da_w_q.shape[2]:].astype(BF16), da_norm=da_norm[0],
        da_w_q=(da_w_q[0] * (float(dk) ** -0.5 * LOG2E)).astype(BF16),
        da_lq1=da_lq1[0], da_lk1=da_lk1[0], da_lq2=da_lq2[0], da_lk2=da_lk2[0], da_subln=da_subln[0],
        da_w_o=da_w_o[0].astype(BF16), ffn_norm0=ffn_norm[0], ffn_norm1=ffn_norm[1],
        mlp_w_gate=mlp_w_gate[0].astype(BF16), mlp_w_up=mlp_w_up[0].astype(BF16), mlp_w_down=mlp_w_down[0].astype(BF16),
        moe_router=moe_router[0], moe_w_gate=moe_w_gate[0].astype(BF16), moe_w_up=moe_w_up[0].astype(BF16),
        moe_w_down=moe_w_down[0].astype(BF16), final_norm=final_norm,
        da_heads=nh_da, lam_init=0.8 - 0.6 * math.exp(-0.3 * layer),
        slopes=jnp.exp2(-8.0 * jnp.arange(1, nh_da + 1, dtype=F32) / nh_da),
    )
    bp = x_prompt.shape[0]
    _, _, nh_ml, hd, _ = state_mlstm_C.shape
    kc = state_conv.shape[2]
    inner = state_conv.shape[3]
    zeros = lambda *s: jnp.zeros(s, F32)
    out_p = _run_group(x_prompt, zeros(bp, nh_ml, hd, hd), zeros(bp, nh_ml, hd), zeros(bp, nh_ml),
                       zeros(bp, kc, inner), None, w)
    pool, page = cache_k.shape[0], cache_k.shape[1]
    cache_kt = jnp.transpose(cache_k, (0, 2, 3, 4, 1)).reshape(pool, -1, page)
    past = (cache_kt, cache_v, page_table)
    out_s = _run_group(x_sample, state_mlstm_C[0], state_mlstm_n[0], state_mlstm_m[0], state_conv[0], past, w)
    y_p, p_c, p_n, p_m, p_conv, p_k, p_v = out_p
    y_s, s_c, s_n, s_m, s_conv, s_k, s_v = out_s
    return (y_p, y_s, p_c, p_n, p_m, p_conv, p_k, p_v, s_c, s_n, s_m, s_conv, s_k, s_v)
```

```python
import functools
import math

import jax
import jax.numpy as jnp
from jax import lax
from jax.experimental import pallas as pl
from jax.experimental.pallas import tpu as pltpu

F32 = jnp.float32
BF16 = jnp.bfloat16
EPS = 1e-6
TOP_K = 2
LOG2E = 1.4426950408889634
LANES = 128
SUBLANES = 8
VMEM_LIMIT_BYTES = 56 * 2**20


def _tile(dim, pref, align):
    t = (min(pref, dim) // align) * align
    while t >= align:
        if dim % t == 0:
            return t
        t -= align
    return dim


def _cparams(*sem):
    return pltpu.CompilerParams(dimension_semantics=sem, vmem_limit_bytes=VMEM_LIMIT_BYTES)


def _dot(a, b):
    return jnp.dot(a, b, preferred_element_type=F32)


def _dot_nt(a, b):
    return lax.dot_general(a, b, (((1,), (1,)), ((), ())), preferred_element_type=F32)


def _dot_tn(a, b):
    return lax.dot_general(a, b, (((0,), (0,)), ((), ())), preferred_element_type=F32)


def _rms(x, g):
    return x * lax.rsqrt(jnp.mean(x * x, axis=-1, keepdims=True) + EPS) * g


def _silu(x):
    return x / (1.0 + jnp.exp(-x))


def _rms_matmul_body(x_ref, g_ref, w_ref, o_ref, xn_ref):
    @pl.when(pl.program_id(1) == 0)
    def _():
        xn_ref[...] = _rms(x_ref[...], g_ref[...]).astype(BF16)

    o_ref[...] = _dot(xn_ref[...], w_ref[...]).astype(o_ref.dtype)


def rms_matmul(x, g, w, out_dtype):
    n, d = x.shape
    f = w.shape[1]
    tm = _tile(n, 2048, SUBLANES)
    tn = _tile(f, 1024 if f <= 1024 else 512, LANES)
    return pl.pallas_call(
        _rms_matmul_body,
        grid=(n // tm, f // tn),
        in_specs=[
            pl.BlockSpec((tm, d), lambda i, j: (i, 0)),
            pl.BlockSpec((1, d), lambda i, j: (0, 0)),
            pl.BlockSpec((d, tn), lambda i, j: (0, j)),
        ],
        out_specs=pl.BlockSpec((tm, tn), lambda i, j: (i, j)),
        out_shape=jax.ShapeDtypeStruct((n, f), out_dtype),
        scratch_shapes=[pltpu.VMEM((tm, d), BF16)],
        compiler_params=_cparams("parallel", "arbitrary"),
    )(x, g.reshape(1, d), w)


def _matmul_body(a_ref, w_ref, o_ref):
    o_ref[...] = _dot(a_ref[...], w_ref[...]).astype(o_ref.dtype)


def matmul(a, w, out_dtype, col_block=0):
    n = a.shape[0]
    k, f = w.shape
    tm = _tile(n, 2048, SUBLANES)
    tn = _tile(f, 512, LANES)
    return pl.pallas_call(
        _matmul_body,
        grid=(n // tm, f // tn),
        in_specs=[
            pl.BlockSpec((tm, k), lambda i, j: (i, col_block)),
            pl.BlockSpec((k, tn), lambda i, j: (0, j)),
        ],
        out_specs=pl.BlockSpec((tm, tn), lambda i, j: (i, j)),
        out_shape=jax.ShapeDtypeStruct((n, f), out_dtype),
        compiler_params=_cparams("parallel", "parallel"),
    )(a, w)


def _matmul_res_body(a_ref, w_ref, r_ref, o_ref):
    o_ref[...] = r_ref[...] + _dot(a_ref[...], w_ref[...])


def matmul_res(a, w, res):
    n, k = a.shape
    f = w.shape[1]
    tm = _tile(n, 512, SUBLANES)
    return pl.pallas_call(
        _matmul_res_body,
        grid=(n // tm,),
        in_specs=[
            pl.BlockSpec((tm, k), lambda i: (i, 0)),
            pl.BlockSpec((k, f), lambda i: (0, 0)),
            pl.BlockSpec((tm, f), lambda i: (i, 0)),
        ],
        out_specs=pl.BlockSpec((tm, f), lambda i: (i, 0)),
        out_shape=jax.ShapeDtypeStruct((n, f), F32),
        compiler_params=_cparams("parallel"),
    )(a, w, res)


def _conv_body(x_ref, c0_ref, w_ref, b_ref, xc_ref, cn_ref, buf_ref, *, bb, tb, kc, cc):
    t = pl.program_id(1)
    lo = SUBLANES - (kc - 1)
    c = x_ref.shape[2]
    for s in range(bb):
        @pl.when(t == 0)
        def _():
            buf_ref[s, lo:SUBLANES, :] = c0_ref[s]

        @pl.when(t > 0)
        def _():
            buf_ref[s, lo:SUBLANES, :] = buf_ref[s, tb + lo:tb + SUBLANES, :]

        buf_ref[s, SUBLANES:SUBLANES + tb, :] = x_ref[s].astype(F32)
        for c0 in range(0, c, cc):
            y = b_ref[:, c0:c0 + cc]
            for i in range(kc):
                y = y + buf_ref[s, lo + i:lo + i + tb, c0:c0 + cc] * w_ref[i:i + 1, c0:c0 + cc]
            xc_ref[s, :, c0:c0 + cc] = _silu(y).astype(BF16)
        cn_ref[s] = buf_ref[s, tb + lo:tb + SUBLANES, :]


def conv_silu(up, conv0, w, b):
    bsz, t, c2 = up.shape
    c = c2 // 2
    kc = w.shape[0]
    tb = _tile(t, 512, SUBLANES)
    bb = _tile(bsz, max(1, 64 // tb), 1) if tb == t else 1
    cc = _tile(c, 512, LANES)
    body = functools.partial(_conv_body, bb=bb, tb=tb, kc=kc, cc=cc)
    return pl.pallas_call(
        body,
        grid=(bsz // bb, t // tb),
        in_specs=[
            pl.BlockSpec((bb, tb, c), lambda i, j: (i, j, 0)),
            pl.BlockSpec((bb, kc - 1, c), lambda i, j: (i, 0, 0)),
            pl.BlockSpec((kc, c), lambda i, j: (0, 0)),
            pl.BlockSpec((1, c), lambda i, j: (0, 0)),
        ],
        out_specs=[
            pl.BlockSpec((bb, tb, c), lambda i, j: (i, j, 0)),
            pl.BlockSpec((bb, kc - 1, c), lambda i, j: (i, 0, 0)),
        ],
        out_shape=[
            jax.ShapeDtypeStruct((bsz, t, c), BF16),
            jax.ShapeDtypeStruct((bsz, kc - 1, c), F32),
        ],
        scratch_shapes=[pltpu.VMEM((bb, tb + SUBLANES, c), F32)],
        compiler_params=_cparams("parallel", "arbitrary"),
    )(up, conv0, w, b.reshape(1, c))


def _gate_body(qk_ref, v_ref, w_ref, b_ref, o_ref, *, nh):
    kq = qk_ref.shape[1]
    g = _dot(qk_ref[...], w_ref[0:kq, :]) + _dot(v_ref[...], w_ref[kq:, :]) + b_ref[...]
    lane = lax.broadcasted_iota(jnp.int32, g.shape, 1)
    ls = jnp.minimum(g, 0.0) - jnp.log(1.0 + jnp.exp(-jnp.abs(g)))
    o_ref[...] = jnp.where(lane >= nh, ls, g)


def mlstm_gates(qk, v, w_ig, b_ig, w_fg, b_fg):
    n = qk.shape[0]
    nh = w_ig.shape[1]
    kin = w_ig.shape[0]
    w = jnp.zeros((kin, LANES), F32).at[:, :nh].set(w_ig).at[:, nh:2 * nh].set(w_fg).astype(BF16)
    b = jnp.zeros((1, LANES), F32).at[0, :nh].set(b_ig).at[0, nh:2 * nh].set(b_fg)
    tm = _tile(n, 512, SUBLANES)
    return pl.pallas_call(
        functools.partial(_gate_body, nh=nh),
        grid=(n // tm,),
        in_specs=[
            pl.BlockSpec((tm, qk.shape[1]), lambda i: (i, 0)),
            pl.BlockSpec((tm, v.shape[1]), lambda i: (i, 0)),
            pl.BlockSpec((kin, LANES), lambda i: (0, 0)),
            pl.BlockSpec((1, LANES), lambda i: (0, 0)),
        ],
        out_specs=pl.BlockSpec((tm, LANES), lambda i: (i, 0)),
        out_shape=jax.ShapeDtypeStruct((n, LANES), F32),
        compiler_params=_cparams("parallel"),
    )(qk, v, w, b)


def _scan_body(q_ref, k_ref, v_ref, gc_ref, gr_ref, c0_ref, n0_ref, m0_ref, gh_ref,
               hn_ref, c_ref, n_ref, m_ref, *, chunk, scale, hb, dk, dv):
    @pl.when(pl.program_id(2) == 0)
    def _():
        c_ref[...] = c0_ref[...]
        n_ref[...] = n0_ref[...]
        m_ref[...] = m0_ref[...]

    row = lax.broadcasted_iota(jnp.int32, (chunk, chunk), 0)
    col = lax.broadcasted_iota(jnp.int32, (chunk, chunk), 1)
    tril = col <= row
    for hh in range(hb):
        q = q_ref[:, hh * dk:(hh + 1) * dk]
        k = k_ref[:, hh * dk:(hh + 1) * dk]
        v = v_ref[:, hh * dv:(hh + 1) * dv]
        ig_c = gc_ref[hh, :, 0:1]
        lf_c = gc_ref[hh, :, 1:2]
        ig_r = gr_ref[hh, 0:1, :]
        lf_r = gr_ref[hh, 1:2, :]
        b_c = jnp.sum(jnp.where(tril, lf_r, 0.0), axis=1, keepdims=True)
        b_r = jnp.sum(jnp.where(row <= col, lf_c, 0.0), axis=0, keepdims=True)
        m_prev = m_ref[hh, 0:1, 0:1]
        d_log = jnp.where(tril, b_c - b_r + ig_r, -jnp.inf)
        inter = b_c + m_prev
        m_t = jnp.maximum(inter, jnp.max(d_log, axis=1, keepdims=True))
        dw = jnp.exp(d_log - m_t)
        w_inter = jnp.exp(inter - m_t)
        s = _dot_nt(q, k) * (dw * scale)
        num = w_inter * _dot(q, c_ref[hh].astype(BF16)) + _dot(s.astype(BF16), v)
        qn = jnp.sum(q.astype(F32) * n_ref[hh], axis=1, keepdims=True)
        den = w_inter * qn + jnp.sum(s, axis=1, keepdims=True)
        h = num / jnp.maximum(jnp.abs(den), jnp.exp(-m_t))
        hn_ref[:, hh * dv:(hh + 1) * dv] = _rms(h, gh_ref[hh]).astype(BF16)

        b_last = b_c[chunk - 1:chunk, :]
        m_last = m_t[chunk - 1:chunk, :]
        w_last = jnp.exp(b_last - b_c + ig_c - m_last)
        scale0 = jnp.exp(b_last + m_prev - m_last)
        kw = k.astype(F32) * (w_last * scale)
        c_ref[hh] = scale0 * c_ref[hh] + _dot_tn(kw.astype(BF16), v)
        n_ref[hh] = scale0 * n_ref[hh] + jnp.sum(kw, axis=0, keepdims=True)
        m_ref[hh] = jnp.broadcast_to(m_last, (1, LANES))


def mlstm_scan(qk, v, gates, c0, n0, m0, g_head):
    bsz, t, _ = v.shape
    _, nh, dk, dv = c0.shape
    chunk = _tile(t, 256, LANES) if t % LANES == 0 else t
    gi = gates[:, :, :nh]
    gf = gates[:, :, nh:2 * nh]
    g_col = jnp.stack([gi, gf], axis=-1).transpose(0, 2, 1, 3)
    g_row = jnp.stack([gi, gf], axis=-1).transpose(0, 2, 3, 1)
    m0b = jnp.broadcast_to(m0[:, :, None, None], (bsz, nh, 1, LANES))
    hb = nh if t == chunk else 1
    ng = nh // hb
    body = functools.partial(_scan_body, chunk=chunk, scale=float(dk) ** -0.5, hb=hb, dk=dk, dv=dv)
    hn, c, n, m = pl.pallas_call(
        body,
        grid=(bsz, ng, t // chunk),
        in_specs=[
            pl.BlockSpec((None, chunk, hb * dk), lambda b, h, c: (b, c, h)),
            pl.BlockSpec((None, chunk, hb * dk), lambda b, h, c: (b, c, ng + h)),
            pl.BlockSpec((None, chunk, hb * dv), lambda b, h, c: (b, c, h)),
            pl.BlockSpec((None, hb, chunk, 2), lambda b, h, c: (b, h, c, 0)),
            pl.BlockSpec((None, hb, 2, chunk), lambda b, h, c: (b, h, 0, c)),
            pl.BlockSpec((None, hb, dk, dv), lambda b, h, c: (b, h, 0, 0)),
            pl.BlockSpec((None, hb, 1, dk), lambda b, h, c: (b, h, 0, 0)),
            pl.BlockSpec((None, hb, 1, LANES), lambda b, h, c: (b, h, 0, 0)),
            pl.BlockSpec((hb, 1, dv), lambda b, h, c: (h, 0, 0)),
        ],
        out_specs=[
            pl.BlockSpec((None, chunk, hb * dv), lambda b, h, c: (b, c, h)),
            pl.BlockSpec((None, hb, dk, dv), lambda b, h, c: (b, h, 0, 0)),
            pl.BlockSpec((None, hb, 1, dk), lambda b, h, c: (b, h, 0, 0)),
            pl.BlockSpec((None, hb, 1, LANES), lambda b, h, c: (b, h, 0, 0)),
        ],
        out_shape=[
            jax.ShapeDtypeStruct((bsz, t, nh * dv), BF16),
            jax.ShapeDtypeStruct((bsz, nh, dk, dv), F32),
            jax.ShapeDtypeStruct((bsz, nh, 1, dk), F32),
            jax.ShapeDtypeStruct((bsz, nh, 1, LANES), F32),
        ],
        compiler_params=_cparams("parallel", "parallel", "arbitrary"),
    )(qk, qk, v, g_col, g_row, c0, n0.reshape(bsz, nh, 1, dk), m0b, g_head.reshape(nh, 1, dv))
    return hn, c, n.reshape(bsz, nh, dk), m[:, :, 0, 0]


def _down_body(hn_ref, xc_ref, z_ref, skip_ref, w_ref, r_ref, o_ref):
    z = z_ref[...].astype(F32)
    a = (hn_ref[...].astype(F32) + skip_ref[...] * xc_ref[...].astype(F32)) * _silu(z)
    o_ref[...] = r_ref[...] + _dot(a.astype(BF16), w_ref[...])


def gated_down(hn, xc, up, skip, w, res):
    n, c = hn.shape
    d = w.shape[1]
    tm = _tile(n, 512, SUBLANES)
    return pl.pallas_call(
        _down_body,
        grid=(n // tm,),
        in_specs=[
            pl.BlockSpec((tm, c), lambda i: (i, 0)),
            pl.BlockSpec((tm, c), lambda i: (i, 0)),
            pl.BlockSpec((tm, c), lambda i: (i, 1)),
            pl.BlockSpec((1, c), lambda i: (0, 0)),
            pl.BlockSpec((c, d), lambda i: (0, 0)),
            pl.BlockSpec((tm, d), lambda i: (i, 0)),
        ],
        out_specs=pl.BlockSpec((tm, d), lambda i: (i, 0)),
        out_shape=jax.ShapeDtypeStruct((n, d), F32),
        compiler_params=_cparams("parallel"),
    )(hn, xc, up, skip.reshape(1, c), w, res)


def _ffn_body(x_ref, g_ref, wg_ref, wu_ref, wd_ref, o_ref, xn_ref):
    @pl.when(pl.program_id(1) == 0)
    def _():
        x = x_ref[...]
        xn_ref[...] = _rms(x, g_ref[...]).astype(BF16)
        o_ref[...] = x

    xn = xn_ref[...]
    hmid = _silu(_dot(xn, wg_ref[...])) * _dot(xn, wu_ref[...])
    o_ref[...] += _dot(hmid.astype(BF16), wd_ref[...])


def ffn_dense(x, g, wg, wu, wd):
    n, d = x.shape
    f = wg.shape[1]
    tm = _tile(n, 512, SUBLANES)
    tf = _tile(f, 1408, LANES)
    return pl.pallas_call(
        _ffn_body,
        grid=(n // tm, f // tf),
        in_specs=[
            pl.BlockSpec((tm, d), lambda i, j: (i, 0)),
            pl.BlockSpec((1, d), lambda i, j: (0, 0)),
            pl.BlockSpec((d, tf), lambda i, j: (0, j)),
            pl.BlockSpec((d, tf), lambda i, j: (0, j)),
            pl.BlockSpec((tf, d), lambda i, j: (j, 0)),
        ],
        out_specs=pl.BlockSpec((tm, d), lambda i, j: (i, 0)),
        out_shape=jax.ShapeDtypeStruct((n, d), F32),
        scratch_shapes=[pltpu.VMEM((tm, d), BF16)],
        compiler_params=_cparams("parallel", "arbitrary"),
    )(x, g.reshape(1, d), wg, wu, wd)


def _kv_body(x_ref, g_ref, wk_ref, wv_ref, k32_ref, v32_ref, kb_ref, vb_ref, *, feature_major):
    xn = _rms(x_ref[...], g_ref[...]).astype(BF16)
    k = _dot_nt(wk_ref[...], xn) if feature_major else _dot(xn, wk_ref[...])
    v = _dot(xn, wv_ref[...])
    k32_ref[...] = k
    v32_ref[...] = v
    kb_ref[...] = k.astype(BF16)
    vb_ref[...] = v.astype(BF16)


def shared_kv(x, g, wk, wv, seq_len=None):
    n, d = x.shape
    kw = wk.shape[0] if seq_len else wk.shape[1]
    vw = wv.shape[1]
    tm = _tile(seq_len or n, 512, LANES if seq_len else SUBLANES)
    row_spec = lambda width: pl.BlockSpec((tm, width), lambda i: (i, 0))
    if seq_len:
        steps = seq_len // tm
        k_spec = pl.BlockSpec((None, kw, tm), lambda i: (i // steps, 0, i % steps))
        k_shape = (n // seq_len, kw, seq_len)
    else:
        k_spec, k_shape = row_spec(kw), (n, kw)
    return pl.pallas_call(
        functools.partial(_kv_body, feature_major=bool(seq_len)),
        grid=(n // tm,),
        in_specs=[
            row_spec(d),
            pl.BlockSpec((1, d), lambda i: (0, 0)),
            pl.BlockSpec(wk.shape, lambda i: (0, 0)),
            pl.BlockSpec(wv.shape, lambda i: (0, 0)),
        ],
        out_specs=[k_spec, row_spec(vw), k_spec, row_spec(vw)],
        out_shape=[
            jax.ShapeDtypeStruct(k_shape, F32),
            jax.ShapeDtypeStruct((n, vw), F32),
            jax.ShapeDtypeStruct(k_shape, BF16),
            jax.ShapeDtypeStruct((n, vw), BF16),
        ],
        compiler_params=_cparams("parallel"),
    )(x, g.reshape(1, d), wk, wv)


def _lambda_full(lq1_ref, lk1_ref, lq2_ref, lk2_ref, lam_init):
    a = jnp.sum(lq1_ref[...] * lk1_ref[...], axis=1, keepdims=True)
    b = jnp.sum(lq2_ref[...] * lk2_ref[...], axis=1, keepdims=True)
    return jnp.exp(a) - jnp.exp(b) + lam_init


def _attn_prefill_body(slopes_ref, q_ref, k_ref, v_ref, lq1_ref, lk1_ref, lq2_ref, lk2_ref, gs_ref,
                       o_ref, t_ref, m_ref, al_ref, l_ref, acc_ref, *, tq, dk, lam_init):
    h = pl.program_id(1)
    i = pl.program_id(2)
    nt = tq // LANES
    slope = slopes_ref[h] * LOG2E
    q = q_ref[...]
    lane = lax.broadcasted_iota(jnp.int32, q.shape, 1)
    zero = jnp.zeros_like(q)
    qz = (jnp.where(lane < dk, q, zero), jnp.where(lane >= dk, q, zero))
    r = lax.broadcasted_iota(jnp.int32, (tq, tq), 0)
    c = lax.broadcasted_iota(jnp.int32, (tq, tq), 1)
    rel = r - c
    key_bias = slope * lax.broadcasted_iota(jnp.int32, (1, tq), 1).astype(F32)
    m_ref[1] = jnp.full(m_ref.shape[1:], -jnp.inf, F32)
    l_ref[...] = jnp.zeros(l_ref.shape, F32)
    acc_ref[...] = jnp.zeros(acc_ref.shape, F32)

    def block_bias(j):
        return (-slope) * ((i - j) * tq).astype(F32)

    def scores(j, masked, src, dst):
        kb = k_ref[:, pl.ds(pl.multiple_of(j * tq, tq), tq)]
        cj = block_bias(j)
        for mp in range(2):
            t = _dot(qz[mp], kb) + key_bias
            if masked:
                t = jnp.where(rel >= 0, t, -jnp.inf)
            t_ref[dst, mp] = t
            tm = t[:, 0:LANES]
            for ct in range(1, nt):
                tm = jnp.maximum(tm, t[:, ct * LANES:(ct + 1) * LANES])
            m_old = m_ref[src, mp]
            m_new = jnp.maximum(m_old, jnp.max(tm, axis=1, keepdims=True) + cj)
            al_ref[dst, mp] = jnp.exp2(m_old - m_new)
            m_ref[dst, mp] = m_new

    def accumulate(j, slot):
        vb = v_ref[pl.ds(pl.multiple_of(j * tq, tq), tq), :]
        cj = block_bias(j)
        for mp in range(2):
            mb = m_ref[slot, mp] - cj
            alpha = al_ref[slot, mp]
            ps = []
            lsum = None
            for ct in range(nt):
                pc = jnp.exp2((t_ref[slot, mp, :, ct * LANES:(ct + 1) * LANES] - mb).astype(BF16))
                lsum = pc if lsum is None else lsum + pc
                ps.append(pc)
            l_ref[mp] = alpha * l_ref[mp] + lsum.astype(F32)
            a_acc = alpha if acc_ref.shape[2] == LANES else alpha[:, 0:1]
            acc_ref[mp] = a_acc * acc_ref[mp] + _dot(jnp.concatenate(ps, axis=1), vb)

    @pl.when(i == 0)
    def _():
        scores(0, True, 1, 0)
        accumulate(0, 0)

    @pl.when(i > 0)
    def _():
        scores(0, False, 1, 0)

    n_plain = jnp.maximum(i - 1, 0)

    def pair(jj, carry):
        j = 2 * jj
        accumulate(j, 0)
        scores(j + 1, False, 0, 1)
        accumulate(j + 1, 1)
        scores(j + 2, False, 1, 0)
        return carry

    lax.fori_loop(0, n_plain // 2, pair, 0)

    @pl.when(n_plain % 2 == 1)
    def _():
        accumulate(n_plain - 1, 0)
        scores(n_plain, False, 0, 1)

    @pl.when((i > 0) & (i % 2 == 1))
    def _():
        accumulate(i - 1, 0)
        scores(i, True, 0, 1)
        accumulate(i, 1)

    @pl.when((i > 0) & (i % 2 == 0))
    def _():
        accumulate(i - 1, 1)
        scores(i, True, 1, 0)
        accumulate(i, 0)

    lam = _lambda_full(lq1_ref, lk1_ref, lq2_ref, lk2_ref, lam_init)
    l0 = jnp.sum(l_ref[0], axis=1, keepdims=True)
    l1 = jnp.sum(l_ref[1], axis=1, keepdims=True)
    o = acc_ref[0] / l0 - lam * (acc_ref[1] / l1)
    o_ref[...] = (_rms(o, gs_ref[...]) * (1.0 - lam_init)).astype(BF16)


def attn_prefill(q, k, v, slopes, lq1, lk1, lq2, lk2, g_sub, nh, lam_init):
    bsz, t, qw = q.shape
    dk2 = qw // nh
    dv = v.shape[2] // nh
    tq = _tile(t, 512, LANES)
    body = functools.partial(_attn_prefill_body, tq=tq, dk=dk2 // 2, lam_init=lam_init)
    vec = lambda a: a.reshape(1, -1)
    small = lambda n: pl.BlockSpec((1, n), lambda b, h, i: (0, 0))
    return pl.pallas_call(
        body,
        grid=(bsz, nh, t // tq),
        in_specs=[
            pl.BlockSpec(memory_space=pltpu.SMEM),
            pl.BlockSpec((None, tq, dk2), lambda b, h, i: (b, i, h)),
            pl.BlockSpec((None, dk2, t), lambda b, h, i: (b, h, 0)),
            pl.BlockSpec((None, t, dv), lambda b, h, i: (b, 0, h)),
            small(dk2 // 2), small(dk2 // 2), small(dk2 // 2), small(dk2 // 2), small(dv),
        ],
        out_specs=pl.BlockSpec((None, tq, dv), lambda b, h, i: (b, i, h)),
        out_shape=jax.ShapeDtypeStruct((bsz, t, nh * dv), BF16),
        scratch_shapes=[pltpu.VMEM((2, 2, tq, tq), F32), pltpu.VMEM((2, 2, tq, LANES), F32),
                        pltpu.VMEM((2, 2, tq, LANES), F32), pltpu.VMEM((2, tq, LANES), F32),
                        pltpu.VMEM((2, tq, dv), F32)],
        compiler_params=_cparams("parallel", "parallel", "arbitrary"),
    )(slopes, q, k, v, vec(lq1), vec(lk1), vec(lq2), vec(lk2), vec(g_sub))


def _attn_paged_body(pt_ref, q_ref, *refs, pp, nh, dk, dv, t, page, past, lam_init):
    kp_refs = refs[:pp]
    vp_refs = refs[pp:2 * pp]
    (kn_ref, vn_ref, lq1_ref, lk1_ref, lq2_ref, lk2_ref, gs_ref,
     o_ref, qbd_ref, m_ref, l_ref, acc_ref) = refs[2 * pp:]
    j = pl.program_id(1)
    nj = pl.num_programs(1)
    rows = nh * 2 * t
    hr = 2 * t
    qw = nh * 2 * dk

    @pl.when(j == 0)
    def _():
        qf = q_ref[...].astype(F32)
        qrep = jnp.concatenate([qf] * (2 * nh), axis=0)
        rr = lax.broadcasted_iota(jnp.int32, (rows, qw), 0)
        cc = lax.broadcasted_iota(jnp.int32, (rows, qw), 1)
        qbd_ref[...] = jnp.where(rr // t == cc // dk, qrep, 0.0).astype(BF16)
        m_ref[...] = jnp.full(m_ref.shape, -jnp.inf, F32)
        l_ref[...] = jnp.zeros(l_ref.shape, F32)
        acc_ref[...] = jnp.zeros(acc_ref.shape, F32)

    ri = lax.broadcasted_iota(jnp.int32, (rows, 1), 0)
    head = (ri // hr).astype(F32)
    slope = jnp.exp2(-8.0 * (head + 1.0) / nh) * LOG2E
    qpos = past + ri % t
    kl = lax.broadcasted_iota(jnp.int32, (rows, page), 1)

    def update(blocks):
        ss = []
        for s, _, kpos, valid in blocks:
            dist = qpos - kpos
            s = s - slope * dist.astype(F32)
            if valid is not None:
                s = jnp.where(valid & (dist >= 0), s, -jnp.inf)
            ss.append(s)
        s = jnp.concatenate(ss, axis=1)
        m = m_ref[...]
        m_new = jnp.maximum(m, jnp.max(s, axis=1, keepdims=True))
        p = jnp.exp2(s - m_new)
        alpha = jnp.exp2(m - m_new)
        l_ref[...] = alpha * l_ref[...] + jnp.sum(p, axis=1, keepdims=True)
        m_ref[...] = m_new
        pb = p.astype(BF16)
        for hh in range(nh):
            rs = slice(hh * hr, (hh + 1) * hr)
            vh = jnp.concatenate([blk[1](hh) for blk in blocks], axis=0)
            acc_ref[rs, :] = alpha[rs] * acc_ref[rs, :] + _dot(pb[rs], vh)

    def past_blocks():
        out = []
        for r in range(pp):
            s = _dot(qbd_ref[...], kp_refs[r][...].astype(BF16))
            v_head = lambda hh, r=r: vp_refs[r][pl.ds(hh, page, stride=nh), :].astype(BF16)
            out.append((s, v_head, (j * pp + r) * page + kl, None))
        return out

    @pl.when(j < nj - 1)
    def _():
        update(past_blocks())

    @pl.when(j == nj - 1)
    def _():
        new = (_dot_nt(qbd_ref[...], kn_ref[...]), lambda hh: vn_ref[:, hh * dv:(hh + 1) * dv], past + kl, kl < t)
        update(past_blocks() + [new])
        o = acc_ref[...] / l_ref[...]
        lam = _lambda_full(lq1_ref, lk1_ref, lq2_ref, lk2_ref, lam_init)
        for hh in range(nh):
            r0 = hh * hr
            od = o[r0:r0 + t] - lam * o[r0 + t:r0 + hr]
            o_ref[:, hh * dv:(hh + 1) * dv] = (_rms(od, gs_ref[...]) * (1.0 - lam_init)).astype(BF16)


def attn_paged(q, cache_kt, cache_v, page_table, k_new, v_new, lq1, lk1, lq2, lk2, g_sub, lam_init):
    bsz, t, qw = q.shape
    _, page, nh, dv = cache_v.shape
    vw = nh * dv
    dk = qw // (2 * nh)
    n_pages = page_table.shape[1]
    pp = 4 if n_pages % 4 == 0 else (2 if n_pages % 2 == 0 else 1)
    rows = nh * 2 * t
    kn = jnp.pad(k_new, ((0, 0), (0, page - t), (0, 0)))
    vn = jnp.pad(v_new, ((0, 0), (0, page - t), (0, 0)))
    body = functools.partial(_attn_paged_body, pp=pp, nh=nh, dk=dk, dv=dv, t=t, page=page,
                             past=n_pages * page, lam_init=lam_init)
    vec = lambda a: a.reshape(1, -1)
    small = lambda n: pl.BlockSpec((1, n), lambda b, j, pt: (0, 0))
    k_spec = lambda r: pl.BlockSpec((None, qw, page), lambda b, j, pt: (pt[b, j * pp + r], 0, 0))
    v_spec = lambda r: pl.BlockSpec((None, page * nh, dv), lambda b, j, pt: (pt[b, j * pp + r], 0, 0))
    grid_spec = pltpu.PrefetchScalarGridSpec(
        num_scalar_prefetch=1,
        grid=(bsz, n_pages // pp),
        in_specs=(
            [pl.BlockSpec((None, t, qw), lambda b, j, pt: (b, 0, 0))]
            + [k_spec(r) for r in range(pp)]
            + [v_spec(r) for r in range(pp)]
            + [pl.BlockSpec((None, page, qw), lambda b, j, pt: (b, 0, 0)),
               pl.BlockSpec((None, page, vw), lambda b, j, pt: (b, 0, 0)),
               small(dk), small(dk), small(dk), small(dk), small(dv)]
        ),
        out_specs=pl.BlockSpec((None, t, vw), lambda b, j, pt: (b, 0, 0)),
        scratch_shapes=[
            pltpu.VMEM((rows, qw), BF16),
            pltpu.VMEM((rows, 1), F32),
            pltpu.VMEM((rows, 1), F32),
            pltpu.VMEM((rows, dv), F32),
        ],
    )
    return pl.pallas_call(
        body,
        grid_spec=grid_spec,
        out_shape=jax.ShapeDtypeStruct((bsz, t, vw), BF16),
        compiler_params=_cparams("parallel", "arbitrary"),
    )(page_table, q, *([cache_kt] * pp), *([cache_v.reshape(-1, page * nh, dv)] * pp), kn, vn,
      vec(lq1), vec(lk1), vec(lq2), vec(lk2), vec(g_sub))


def _to_token_tiles(dst_ref, x, n_rows):
    nch = x.shape[1] // LANES
    for c in range(nch):
        dst_ref[pl.ds(c, n_rows, stride=nch), :] = x[:, c * LANES:(c + 1) * LANES]


def _router_body(x_ref, g_ref, wr_ref, xn_ref, idx_ref, gate_ref, *, ne):
    xn = _rms(x_ref[...], g_ref[...])
    xn_ref[...] = xn
    logits = jnp.dot(xn, wr_ref[...], preferred_element_type=F32, precision=lax.Precision.HIGHEST)
    lane = lax.broadcasted_iota(jnp.int32, logits.shape, 1)
    logits = jnp.where(lane < ne, logits, -jnp.inf)
    m1 = jnp.max(logits, axis=1, keepdims=True)
    i1 = jnp.min(jnp.where(logits == m1, lane, LANES), axis=1, keepdims=True)
    rest = jnp.where(lane == i1, -jnp.inf, logits)
    m2 = jnp.max(rest, axis=1, keepdims=True)
    i2 = jnp.min(jnp.where(rest == m2, lane, LANES), axis=1, keepdims=True)
    e = jnp.exp(m2 - m1)
    g1 = 1.0 / (1.0 + e)
    g2 = e / (1.0 + e)
    idx_ref[...] = jnp.where(lane == 0, i1, jnp.where(lane == 1, i2, 0))
    gate_ref[...] = jnp.where(lane == 0, g1, jnp.where(lane == 1, g2, 0.0))


def moe_router(x, g, w_router):
    n, d = x.shape
    ne = w_router.shape[1]
    wr = jnp.zeros((d, LANES), F32).at[:, :ne].set(w_router)
    tm = _tile(n, 512, SUBLANES)
    return pl.pallas_call(
        functools.partial(_router_body, ne=ne),
        grid=(n // tm,),
        in_specs=[
            pl.BlockSpec((tm, d), lambda i: (i, 0)),
            pl.BlockSpec((1, d), lambda i: (0, 0)),
            pl.BlockSpec((d, LANES), lambda i: (0, 0)),
        ],
        out_specs=[
            pl.BlockSpec((tm, d), lambda i: (i, 0)),
            pl.BlockSpec((tm, LANES), lambda i: (i, 0)),
            pl.BlockSpec((tm, LANES), lambda i: (i, 0)),
        ],
        out_shape=[
            jax.ShapeDtypeStruct((n, d), F32),
            jax.ShapeDtypeStruct((n, LANES), jnp.int32),
            jax.ShapeDtypeStruct((n, LANES), F32),
        ],
        compiler_params=_cparams("parallel"),
    )(x, g.reshape(1, d), wr)


DMA_ISSUE_UNROLL = 8


def _dispatch_body(pos_ref, x_ref, xs_init_hbm, xs_hbm, sem, *, tc):
    del xs_init_hbm
    base = pl.program_id(0) * tc

    def issue(r, carry):
        for kk in range(TOP_K):
            p = pos_ref[TOP_K * (base + r) + kk]
            pltpu.make_async_copy(x_ref.at[pl.ds(r, 1)], xs_hbm.at[pl.ds(p, 1)], sem).start()
        return carry

    lax.fori_loop(0, tc, issue, 0, unroll=DMA_ISSUE_UNROLL)
    for kk in range(TOP_K):
        pltpu.make_async_copy(x_ref, xs_hbm.at[pl.ds(0, tc)], sem).wait()


def dispatch_rows(x, pos, n_rows):
    n, d = x.shape
    tc = _tile(n, 256, SUBLANES)
    grid_spec = pltpu.PrefetchScalarGridSpec(
        num_scalar_prefetch=1,
        grid=(n // tc,),
        in_specs=[pl.BlockSpec((tc, d), lambda i, pos: (i, 0)), pl.BlockSpec(memory_space=pl.ANY)],
        out_specs=pl.BlockSpec(memory_space=pl.ANY),
        scratch_shapes=[pltpu.SemaphoreType.DMA(())],
    )
    return pl.pallas_call(
        functools.partial(_dispatch_body, tc=tc),
        grid_spec=grid_spec,
        out_shape=jax.ShapeDtypeStruct((n_rows, d), x.dtype),
        input_output_aliases={2: 0},
        compiler_params=pltpu.CompilerParams(dimension_semantics=("arbitrary",),
                                             vmem_limit_bytes=VMEM_LIMIT_BYTES),
    )(pos, x, jnp.zeros((n_rows, d), x.dtype))


def _moe_ffn_body(te_ref, tv_ref, xs_ref, wg_ref, wu_ref, wd_ref, o_ref, xb_ref, acc_ref):
    i = pl.program_id(0)
    j = pl.program_id(1)
    tm, d = acc_ref.shape
    nch = d // LANES

    @pl.when(tv_ref[i] == 0)
    def _():
        @pl.when(j == 0)
        def _():
            o_ref[...] = jnp.zeros(o_ref.shape, F32)

    @pl.when(tv_ref[i] > 0)
    def _():
        @pl.when(j == 0)
        def _():
            xb_ref[...] = xs_ref[...].astype(BF16)

        xb = xb_ref[...]
        hmid = _silu(_dot(xb, wg_ref[...])) * _dot(xb, wu_ref[...])
        y = _dot(hmid.astype(BF16), wd_ref[...])

        @pl.when(j == 0)
        def _():
            acc_ref[...] = y

        @pl.when(j > 0)
        def _():
            acc_ref[...] += y

        @pl.when(j == pl.num_programs(1) - 1)
        def _():
            _to_token_tiles(o_ref, acc_ref[...], tm)


def moe_ffn(xs, tile_expert, tile_valid, wg, wu, wd, tm):
    r, d = xs.shape
    nch = d // LANES
    f = wg.shape[2]
    tf = _tile(f, 1408, LANES)
    grid_spec = pltpu.PrefetchScalarGridSpec(
        num_scalar_prefetch=2,
        grid=(r // tm, f // tf),
        in_specs=[
            pl.BlockSpec((tm, d), lambda i, j, te, tv: (i, 0)),
            pl.BlockSpec((None, d, tf), lambda i, j, te, tv: (te[i], 0, j)),
            pl.BlockSpec((None, d, tf), lambda i, j, te, tv: (te[i], 0, j)),
            pl.BlockSpec((None, tf, d), lambda i, j, te, tv: (te[i], j, 0)),
        ],
        out_specs=pl.BlockSpec((tm * nch, LANES), lambda i, j, te, tv: (i, 0)),
        scratch_shapes=[pltpu.VMEM((tm, d), BF16), pltpu.VMEM((tm, d), F32)],
    )
    return pl.pallas_call(
        _moe_ffn_body,
        grid_spec=grid_spec,
        out_shape=jax.ShapeDtypeStruct((r * nch, LANES), F32),
        compiler_params=_cparams("parallel", "arbitrary"),
    )(tile_expert, tile_valid, xs, wg, wu, wd)


def _combine_body(pos_ref, x_ref, gate_ref, gf_ref, ys_hbm, o_ref, buf_ref, sem, *, tc, nch):
    i = pl.program_id(0)
    n_steps = pl.num_programs(0)

    def fetch(step, slot):
        base = step * tc

        def issue(r, carry):
            for kk in range(TOP_K):
                p = pos_ref[TOP_K * (base + r) + kk]
                pltpu.make_async_copy(ys_hbm.at[pl.ds(pl.multiple_of(p * nch, nch), nch)],
                                      buf_ref.at[slot, kk, pl.ds(pl.multiple_of(r * nch, nch), nch)],
                                      sem.at[slot]).start()
            return carry

        lax.fori_loop(0, tc, issue, 0, unroll=DMA_ISSUE_UNROLL)

    slot = i % 2

    @pl.when(i == 0)
    def _():
        fetch(0, 0)

    @pl.when(i + 1 < n_steps)
    def _():
        fetch(i + 1, 1 - slot)

    for kk in range(TOP_K):
        pltpu.make_async_copy(ys_hbm.at[pl.ds(0, tc * nch)], buf_ref.at[slot, kk], sem.at[slot]).wait()
    gate = gate_ref[...]
    ys = []
    ssq = jnp.zeros((tc, 1), F32)
    for c in range(nch):
        y = x_ref[:, c * LANES:(c + 1) * LANES]
        for kk in range(TOP_K):
            y = y + gate[:, kk:kk + 1] * buf_ref[slot, kk, pl.ds(c, tc, stride=nch), :]
        ssq = ssq + jnp.sum(y * y, axis=1, keepdims=True)
        ys.append(y)
    scale = lax.rsqrt(ssq / (nch * LANES) + EPS)
    for c in range(nch):
        o_ref[:, c * LANES:(c + 1) * LANES] = ys[c] * scale * gf_ref[:, c * LANES:(c + 1) * LANES]


def moe_combine_norm(x, gates, pos, ys, g_final):
    n, d = x.shape
    nch = d // LANES
    tc = _tile(n, 256, SUBLANES)
    grid_spec = pltpu.PrefetchScalarGridSpec(
        num_scalar_prefetch=1,
        grid=(n // tc,),
        in_specs=[
            pl.BlockSpec((tc, d), lambda i, pos: (i, 0)),
            pl.BlockSpec((tc, LANES), lambda i, pos: (i, 0)),
            pl.BlockSpec((1, d), lambda i, pos: (0, 0)),
            pl.BlockSpec(memory_space=pl.ANY),
        ],
        out_specs=pl.BlockSpec((tc, d), lambda i, pos: (i, 0)),
        scratch_shapes=[pltpu.VMEM((2, TOP_K, tc * nch, LANES), F32), pltpu.SemaphoreType.DMA((2,))],
    )
    return pl.pallas_call(
        functools.partial(_combine_body, tc=tc, nch=nch),
        grid_spec=grid_spec,
        out_shape=jax.ShapeDtypeStruct((n, d), F32),
        compiler_params=pltpu.CompilerParams(dimension_semantics=("arbitrary",),
                                             vmem_limit_bytes=VMEM_LIMIT_BYTES),
    )(pos, x, gates, g_final.reshape(1, d), ys)


def _routing_tables(top_i, ne, tm):
    n = top_i.shape[0]
    e_flat = top_i.reshape(-1)
    onehot = (e_flat[:, None] == jnp.arange(ne, dtype=jnp.int32)[None, :]).astype(jnp.int32)
    csum = jnp.cumsum(onehot, axis=0)
    rank = jnp.sum(onehot * csum, axis=1) - 1
    counts = csum[-1]
    padded = ((counts + tm - 1) // tm) * tm
    pad_end = jnp.cumsum(padded)
    pad_start = pad_end - padded
    pos = jnp.sum(onehot * pad_start[None, :], axis=1) + rank
    n_tiles = (n * TOP_K) // tm + ne
    tile_row0 = jnp.arange(n_tiles, dtype=jnp.int32) * tm
    tile_valid = (tile_row0 < pad_end[-1]).astype(jnp.int32)
    last_e = jnp.max(jnp.where(counts > 0, jnp.arange(ne, dtype=jnp.int32), 0))
    tile_expert = jnp.sum((pad_end[None, :] <= tile_row0[:, None]).astype(jnp.int32), axis=1)
    tile_expert = jnp.minimum(tile_expert, last_e)
    return pos.astype(jnp.int32), tile_expert.astype(jnp.int32), tile_valid


def moe_block_final(x, g_ffn, w_router, wg, wu, wd, g_final):
    n, d = x.shape
    ne = w_router.shape[1]
    tm = _tile(n * TOP_K, 512, SUBLANES)
    xn, idx, gates = moe_router(x, g_ffn, w_router)
    pos, tile_expert, tile_valid = _routing_tables(idx[:, :TOP_K], ne, tm)
    xs = dispatch_rows(xn, pos, tile_expert.shape[0] * tm)
    ys = moe_ffn(xs, tile_expert, tile_valid, wg, wu, wd, tm)
    return moe_combine_norm(x, gates, pos, ys, g_final)


def _run_group(x, c0, n0, m0, conv0, past, w):
    bsz, t, d = x.shape
    n = bsz * t
    nh_ml = c0.shape[1]
    xf = x.reshape(n, d)

    up = rms_matmul(xf, w["ml_norm"], w["ml_w_up"], BF16)
    inner = up.shape[1] // 2
    xc, conv_new = conv_silu(up.reshape(bsz, t, 2 * inner), conv0, w["ml_conv_w"], w["ml_conv_b"])
    xc = xc.reshape(n, inner)
    qk = matmul(xc, w["ml_w_qk"], BF16)
    v = matmul(up, w["ml_w_v"], BF16, col_block=0)
    gates = mlstm_gates(qk, v, w["ml_w_ig"], w["ml_b_ig"], w["ml_w_fg"], w["ml_b_fg"])
    hn, c_new, n_new, m_new = mlstm_scan(qk.reshape(bsz, t, 2 * inner), v.reshape(bsz, t, inner),
                                         gates.reshape(bsz, t, LANES), c0, n0, m0, w["ml_head_norm"])
    x1 = gated_down(hn.reshape(n, inner), xc, up, w["ml_skip"], w["ml_w_down"], xf)
    x2 = ffn_dense(x1, w["ffn_norm0"], w["mlp_w_gate"], w["mlp_w_up"], w["mlp_w_down"])

    nh_da = w["da_heads"]
    kw = w["da_w_q"].shape[1]
    dk = kw // (2 * nh_da)
    long_seq = past is None and t % LANES == 0
    if long_seq:
        k32, v32, kb, vb = shared_kv(x2, w["kv_norm"], w["w_k_t"], w["w_v"], seq_len=t)
        k_leaf = jnp.transpose(k32.reshape(bsz, nh_da, 2, dk, t), (0, 4, 1, 2, 3))
    else:
        k32, v32, kb, vb = shared_kv(x2, w["kv_norm"], w["w_k"], w["w_v"])
        k_leaf = k32.reshape(bsz, t, nh_da, 2, dk)
    vw = v32.shape[1]

    q = rms_matmul(x2, w["da_norm"], w["da_w_q"], BF16).reshape(bsz, t, kw)
    lam_init = w["lam_init"]
    lam_args = (w["da_lq1"], w["da_lk1"], w["da_lq2"], w["da_lk2"], w["da_subln"])
    if past is None:
        kt = kb if long_seq else jnp.transpose(kb.reshape(bsz, t, kw), (0, 2, 1))
        o = attn_prefill(q, kt, vb.reshape(bsz, t, vw), w["slopes"], *lam_args, nh_da, lam_init)
    else:
        cache_kt, cache_v, page_table = past
        o = attn_paged(q, cache_kt, cache_v, page_table, kb.reshape(bsz, t, kw), vb.reshape(bsz, t, vw),
                       *lam_args, lam_init)
    x3 = matmul_res(o.reshape(n, vw), w["da_w_o"], x2)
    y = moe_block_final(x3, w["ffn_norm1"], w["moe_router"], w["moe_w_gate"], w["moe_w_up"], w["moe_w_down"],
                        w["final_norm"])
    return (y.reshape(bsz, t, d), c_new[None], n_new[None], m_new[None], conv_new[None],
            k_leaf, v32.reshape(bsz, t, nh_da, vw // nh_da))


def kernel(x_prompt, x_sample, state_mlstm_C, state_mlstm_n, state_mlstm_m, state_conv, cache_k, cache_v, page_table, ml_norm, ml_w_up, ml_conv_w, ml_conv_b, ml_w_q, ml_w_k, ml_w_v, ml_w_ig, ml_b_ig, ml_w_fg, ml_b_fg, ml_head_norm, ml_skip, ml_w_down, kv_norm, w_kv, da_norm, da_w_q, da_lq1, da_lk1, da_lq2, da_lk2, da_subln, da_w_o, ffn_norm, mlp_w_gate, mlp_w_up, mlp_w_down, moe_router, moe_w_gate, moe_w_up, moe_w_down, final_norm):
    assert ml_norm.shape[0] == 1 and da_norm.shape[0] == 1 and ffn_norm.shape[0] == 2, "one mLSTM layer then one attention layer"
    nh_da = cache_k.shape[2]
    dk = cache_k.shape[4]
    layer = 1
    w = dict(
        ml_norm=ml_norm[0], ml_w_up=ml_w_up[0].astype(BF16), ml_conv_w=ml_conv_w[0], ml_conv_b=ml_conv_b[0],
        ml_w_qk=jnp.concatenate([ml_w_q[0], ml_w_k[0]], axis=1).astype(BF16), ml_w_v=ml_w_v[0].astype(BF16),
        ml_w_ig=ml_w_ig[0], ml_b_ig=ml_b_ig[0], ml_w_fg=ml_w_fg[0], ml_b_fg=ml_b_fg[0],
        ml_head_norm=ml_head_norm[0], ml_skip=ml_skip[0], ml_w_down=ml_w_down[0].astype(BF16),
        kv_norm=kv_norm, w_k=w_kv[:, :da_w_q.shape[2]].astype(BF16), w_k_t=w_kv[:, :da_w_q.shape[2]].T.astype(BF16),
        w_v=w_kv[:, da_w_q.shape[2]:].astype(BF16), da_norm=da_norm[0],
        da_w_q=(da_w_q[0] * (float(dk) ** -0.5 * LOG2E)).astype(BF16),
        da_lq1=da_lq1[0], da_lk1=da_lk1[0], da_lq2=da_lq2[0], da_lk2=da_lk2[0], da_subln=da_subln[0],
        da_w_o=da_w_o[0].astype(BF16), ffn_norm0=ffn_norm[0], ffn_norm1=ffn_norm[1],
        mlp_w_gate=mlp_w_gate[0].astype(BF16), mlp_w_up=mlp_w_up[0].astype(BF16), mlp_w_down=mlp_w_down[0].astype(BF16),
        moe_router=moe_router[0], moe_w_gate=moe_w_gate[0].astype(BF16), moe_w_up=moe_w_up[0].astype(BF16),
        moe_w_down=moe_w_down[0].astype(BF16), final_norm=final_norm,
        da_heads=nh_da, lam_init=0.8 - 0.6 * math.exp(-0.3 * layer),
        slopes=jnp.exp2(-8.0 * jnp.arange(1, nh_da + 1, dtype=F32) / nh_da),
    )
    bp = x_prompt.shape[0]
    _, _, nh_ml, hd, _ = state_mlstm_C.shape
    kc = state_conv.shape[2]
    inner = state_conv.shape[3]
    zeros = lambda *s: jnp.zeros(s, F32)
    out_p = _run_group(x_prompt, zeros(bp, nh_ml, hd, hd), zeros(bp, nh_ml, hd), zeros(bp, nh_ml),
                       zeros(bp, kc, inner), None, w)
    pool, page = cache_k.shape[0], cache_k.shape[1]
    cache_kt = jnp.transpose(cache_k, (0, 2, 3, 4, 1)).reshape(pool, -1, page)
    past = (cache_kt, cache_v, page_table)
    out_s = _run_group(x_sample, state_mlstm_C[0], state_mlstm_n[0], state_mlstm_m[0], state_conv[0], past, w)
    y_p, p_c, p_n, p_m, p_conv, p_k, p_v = out_p
    y_s, s_c, s_n, s_m, s_conv, s_k, s_v = out_s
    return (y_p, y_s, p_c, p_n, p_m, p_conv, p_k, p_v, s_c, s_n, s_m, s_conv, s_k, s_v)
```

```python
import functools
import math

import jax
import jax.numpy as jnp
from jax import lax
from jax.experimental import pallas as pl
from jax.experimental.pallas import tpu as pltpu

F32 = jnp.float32
BF16 = jnp.bfloat16
EPS = 1e-6
TOP_K = 2
LOG2E = 1.4426950408889634
LANES = 128
SUBLANES = 8
VMEM_LIMIT_BYTES = 56 * 2**20


def _tile(dim, pref, align):
    t = (min(pref, dim) // align) * align
    while t >= align:
        if dim % t == 0:
            return t
        t -= align
    return dim


def _cparams(*sem):
    return pltpu.CompilerParams(dimension_semantics=sem, vmem_limit_bytes=VMEM_LIMIT_BYTES)


def _dot(a, b):
    return jnp.dot(a, b, preferred_element_type=F32)


def _dot_nt(a, b):
    return lax.dot_general(a, b, (((1,), (1,)), ((), ())), preferred_element_type=F32)


def _dot_tn(a, b):
    return lax.dot_general(a, b, (((0,), (0,)), ((), ())), preferred_element_type=F32)


def _rms(x, g):
    return x * lax.rsqrt(jnp.mean(x * x, axis=-1, keepdims=True) + EPS) * g


def _silu(x):
    return x / (1.0 + jnp.exp(-x))


def _rms_matmul_body(x_ref, g_ref, w_ref, o_ref, xn_ref):
    @pl.when(pl.program_id(1) == 0)
    def _():
        xn_ref[...] = _rms(x_ref[...], g_ref[...]).astype(BF16)

    o_ref[...] = _dot(xn_ref[...], w_ref[...]).astype(o_ref.dtype)


def rms_matmul(x, g, w, out_dtype):
    n, d = x.shape
    f = w.shape[1]
    tm = _tile(n, 2048, SUBLANES)
    tn = _tile(f, 1024 if f <= 1024 else 512, LANES)
    return pl.pallas_call(
        _rms_matmul_body,
        grid=(n // tm, f // tn),
        in_specs=[
            pl.BlockSpec((tm, d), lambda i, j: (i, 0)),
            pl.BlockSpec((1, d), lambda i, j: (0, 0)),
            pl.BlockSpec((d, tn), lambda i, j: (0, j)),
        ],
        out_specs=pl.BlockSpec((tm, tn), lambda i, j: (i, j)),
        out_shape=jax.ShapeDtypeStruct((n, f), out_dtype),
        scratch_shapes=[pltpu.VMEM((tm, d), BF16)],
        compiler_params=_cparams("parallel", "arbitrary"),
    )(x, g.reshape(1, d), w)


def _matmul_body(a_ref, w_ref, o_ref):
    o_ref[...] = _dot(a_ref[...], w_ref[...]).astype(o_ref.dtype)


def matmul(a, w, out_dtype, col_block=0):
    n = a.shape[0]
    k, f = w.shape
    tm = _tile(n, 2048, SUBLANES)
    tn = _tile(f, 512, LANES)
    return pl.pallas_call(
        _matmul_body,
        grid=(n // tm, f // tn),
        in_specs=[
            pl.BlockSpec((tm, k), lambda i, j: (i, col_block)),
            pl.BlockSpec((k, tn), lambda i, j: (0, j)),
        ],
        out_specs=pl.BlockSpec((tm, tn), lambda i, j: (i, j)),
        out_shape=jax.ShapeDtypeStruct((n, f), out_dtype),
        compiler_params=_cparams("parallel", "parallel"),
    )(a, w)


def _matmul_res_body(a_ref, w_ref, r_ref, o_ref):
    o_ref[...] = r_ref[...] + _dot(a_ref[...], w_ref[...])


def matmul_res(a, w, res):
    n, k = a.shape
    f = w.shape[1]
    tm = _tile(n, 512, SUBLANES)
    return pl.pallas_call(
        _matmul_res_body,
        grid=(n // tm,),
        in_specs=[
            pl.BlockSpec((tm, k), lambda i: (i, 0)),
            pl.BlockSpec((k, f), lambda i: (0, 0)),
            pl.BlockSpec((tm, f), lambda i: (i, 0)),
        ],
        out_specs=pl.BlockSpec((tm, f), lambda i: (i, 0)),
        out_shape=jax.ShapeDtypeStruct((n, f), F32),
        compiler_params=_cparams("parallel"),
    )(a, w, res)


def _conv_body(x_ref, c0_ref, w_ref, b_ref, xc_ref, cn_ref, buf_ref, *, bb, tb, kc, cc):
    t = pl.program_id(1)
    lo = SUBLANES - (kc - 1)
    c = x_ref.shape[2]
    for s in range(bb):
        @pl.when(t == 0)
        def _():
            buf_ref[s, lo:SUBLANES, :] = c0_ref[s]

        @pl.when(t > 0)
        def _():
            buf_ref[s, lo:SUBLANES, :] = buf_ref[s, tb + lo:tb + SUBLANES, :]

        buf_ref[s, SUBLANES:SUBLANES + tb, :] = x_ref[s].astype(F32)
        for c0 in range(0, c, cc):
            y = b_ref[:, c0:c0 + cc]
            for i in range(kc):
                y = y + buf_ref[s, lo + i:lo + i + tb, c0:c0 + cc] * w_ref[i:i + 1, c0:c0 + cc]
            xc_ref[s, :, c0:c0 + cc] = _silu(y).astype(BF16)
        cn_ref[s] = buf_ref[s, tb + lo:tb + SUBLANES, :]


def conv_silu(up, conv0, w, b):
    bsz, t, c2 = up.shape
    c = c2 // 2
    kc = w.shape[0]
    tb = _tile(t, 512, SUBLANES)
    bb = _tile(bsz, max(1, 64 // tb), 1) if tb == t else 1
    cc = _tile(c, 512, LANES)
    body = functools.partial(_conv_body, bb=bb, tb=tb, kc=kc, cc=cc)
    return pl.pallas_call(
        body,
        grid=(bsz // bb, t // tb),
        in_specs=[
            pl.BlockSpec((bb, tb, c), lambda i, j: (i, j, 0)),
            pl.BlockSpec((bb, kc - 1, c), lambda i, j: (i, 0, 0)),
            pl.BlockSpec((kc, c), lambda i, j: (0, 0)),
            pl.BlockSpec((1, c), lambda i, j: (0, 0)),
        ],
        out_specs=[
            pl.BlockSpec((bb, tb, c), lambda i, j: (i, j, 0)),
            pl.BlockSpec((bb, kc - 1, c), lambda i, j: (i, 0, 0)),
        ],
        out_shape=[
            jax.ShapeDtypeStruct((bsz, t, c), BF16),
            jax.ShapeDtypeStruct((bsz, kc - 1, c), F32),
        ],
        scratch_shapes=[pltpu.VMEM((bb, tb + SUBLANES, c), F32)],
        compiler_params=_cparams("parallel", "arbitrary"),
    )(up, conv0, w, b.reshape(1, c))


def _gate_body(qk_ref, v_ref, w_ref, b_ref, o_ref, *, nh):
    kq = qk_ref.shape[1]
    g = _dot(qk_ref[...], w_ref[0:kq, :]) + _dot(v_ref[...], w_ref[kq:, :]) + b_ref[...]
    lane = lax.broadcasted_iota(jnp.int32, g.shape, 1)
    ls = jnp.minimum(g, 0.0) - jnp.log(1.0 + jnp.exp(-jnp.abs(g)))
    o_ref[...] = jnp.where(lane >= nh, ls, g)


def mlstm_gates(qk, v, w_ig, b_ig, w_fg, b_fg):
    n = qk.shape[0]
    nh = w_ig.shape[1]
    kin = w_ig.shape[0]
    w = jnp.zeros((kin, LANES), F32).at[:, :nh].set(w_ig).at[:, nh:2 * nh].set(w_fg).astype(BF16)
    b = jnp.zeros((1, LANES), F32).at[0, :nh].set(b_ig).at[0, nh:2 * nh].set(b_fg)
    tm = _tile(n, 512, SUBLANES)
    return pl.pallas_call(
        functools.partial(_gate_body, nh=nh),
        grid=(n // tm,),
        in_specs=[
            pl.BlockSpec((tm, qk.shape[1]), lambda i: (i, 0)),
            pl.BlockSpec((tm, v.shape[1]), lambda i: (i, 0)),
            pl.BlockSpec((kin, LANES), lambda i: (0, 0)),
            pl.BlockSpec((1, LANES), lambda i: (0, 0)),
        ],
        out_specs=pl.BlockSpec((tm, LANES), lambda i: (i, 0)),
        out_shape=jax.ShapeDtypeStruct((n, LANES), F32),
        compiler_params=_cparams("parallel"),
    )(qk, v, w, b)


def _scan_body(q_ref, k_ref, v_ref, gc_ref, gr_ref, c0_ref, n0_ref, m0_ref, gh_ref,
               hn_ref, c_ref, n_ref, m_ref, *, chunk, scale, hb, dk, dv):
    @pl.when(pl.program_id(2) == 0)
    def _():
        c_ref[...] = c0_ref[...]
        n_ref[...] = n0_ref[...]
        m_ref[...] = m0_ref[...]

    row = lax.broadcasted_iota(jnp.int32, (chunk, chunk), 0)
    col = lax.broadcasted_iota(jnp.int32, (chunk, chunk), 1)
    tril = col <= row
    for hh in range(hb):
        q = q_ref[:, hh * dk:(hh + 1) * dk]
        k = k_ref[:, hh * dk:(hh + 1) * dk]
        v = v_ref[:, hh * dv:(hh + 1) * dv]
        ig_c = gc_ref[hh, :, 0:1]
        lf_c = gc_ref[hh, :, 1:2]
        ig_r = gr_ref[hh, 0:1, :]
        lf_r = gr_ref[hh, 1:2, :]
        b_c = jnp.sum(jnp.where(tril, lf_r, 0.0), axis=1, keepdims=True)
        b_r = jnp.sum(jnp.where(row <= col, lf_c, 0.0), axis=0, keepdims=True)
        m_prev = m_ref[hh, 0:1, 0:1]
        d_log = jnp.where(tril, b_c - b_r + ig_r, -jnp.inf)
        inter = b_c + m_prev
        m_t = jnp.maximum(inter, jnp.max(d_log, axis=1, keepdims=True))
        dw = jnp.exp(d_log - m_t)
        w_inter = jnp.exp(inter - m_t)
        s = _dot_nt(q, k) * (dw * scale)
        num = w_inter * _dot(q, c_ref[hh].astype(BF16)) + _dot(s.astype(BF16), v)
        qn = jnp.sum(q.astype(F32) * n_ref[hh], axis=1, keepdims=True)
        den = w_inter * qn + jnp.sum(s, axis=1, keepdims=True)
        h = num / jnp.maximum(jnp.abs(den), jnp.exp(-m_t))
        hn_ref[:, hh * dv:(hh + 1) * dv] = _rms(h, gh_ref[hh]).astype(BF16)

        b_last = b_c[chunk - 1:chunk, :]
        m_last = m_t[chunk - 1:chunk, :]
        w_last = jnp.exp(b_last - b_c + ig_c - m_last)
        scale0 = jnp.exp(b_last + m_prev - m_last)
        kw = k.astype(F32) * (w_last * scale)
        c_ref[hh] = scale0 * c_ref[hh] + _dot_tn(kw.astype(BF16), v)
        n_ref[hh] = scale0 * n_ref[hh] + jnp.sum(kw, axis=0, keepdims=True)
        m_ref[hh] = jnp.broadcast_to(m_last, (1, LANES))


def mlstm_scan(qk, v, gates, c0, n0, m0, g_head):
    bsz, t, _ = v.shape
    _, nh, dk, dv = c0.shape
    chunk = _tile(t, 256, LANES) if t % LANES == 0 else t
    gi = gates[:, :, :nh]
    gf = gates[:, :, nh:2 * nh]
    g_col = jnp.stack([gi, gf], axis=-1).transpose(0, 2, 1, 3)
    g_row = jnp.stack([gi, gf], axis=-1).transpose(0, 2, 3, 1)
    m0b = jnp.broadcast_to(m0[:, :, None, None], (bsz, nh, 1, LANES))
    hb = nh if t == chunk else 1
    ng = nh // hb
    body = functools.partial(_scan_body, chunk=chunk, scale=float(dk) ** -0.5, hb=hb, dk=dk, dv=dv)
    hn, c, n, m = pl.pallas_call(
        body,
        grid=(bsz, ng, t // chunk),
        in_specs=[
            pl.BlockSpec((None, chunk, hb * dk), lambda b, h, c: (b, c, h)),
            pl.BlockSpec((None, chunk, hb * dk), lambda b, h, c: (b, c, ng + h)),
            pl.BlockSpec((None, chunk, hb * dv), lambda b, h, c: (b, c, h)),
            pl.BlockSpec((None, hb, chunk, 2), lambda b, h, c: (b, h, c, 0)),
            pl.BlockSpec((None, hb, 2, chunk), lambda b, h, c: (b, h, 0, c)),
            pl.BlockSpec((None, hb, dk, dv), lambda b, h, c: (b, h, 0, 0)),
            pl.BlockSpec((None, hb, 1, dk), lambda b, h, c: (b, h, 0, 0)),
            pl.BlockSpec((None, hb, 1, LANES), lambda b, h, c: (b, h, 0, 0)),
            pl.BlockSpec((hb, 1, dv), lambda b, h, c: (h, 0, 0)),
        ],
        out_specs=[
            pl.BlockSpec((None, chunk, hb * dv), lambda b, h, c: (b, c, h)),
            pl.BlockSpec((None, hb, dk, dv), lambda b, h, c: (b, h, 0, 0)),
            pl.BlockSpec((None, hb, 1, dk), lambda b, h, c: (b, h, 0, 0)),
            pl.BlockSpec((None, hb, 1, LANES), lambda b, h, c: (b, h, 0, 0)),
        ],
        out_shape=[
            jax.ShapeDtypeStruct((bsz, t, nh * dv), BF16),
            jax.ShapeDtypeStruct((bsz, nh, dk, dv), F32),
            jax.ShapeDtypeStruct((bsz, nh, 1, dk), F32),
            jax.ShapeDtypeStruct((bsz, nh, 1, LANES), F32),
        ],
        compiler_params=_cparams("parallel", "parallel", "arbitrary"),
    )(qk, qk, v, g_col, g_row, c0, n0.reshape(bsz, nh, 1, dk), m0b, g_head.reshape(nh, 1, dv))
    return hn, c, n.reshape(bsz, nh, dk), m[:, :, 0, 0]


def _down_body(hn_ref, xc_ref, z_ref, skip_ref, w_ref, r_ref, o_ref):
    z = z_ref[...].astype(F32)
    a = (hn_ref[...].astype(F32) + skip_ref[...] * xc_ref[...].astype(F32)) * _silu(z)
    o_ref[...] = r_ref[...] + _dot(a.astype(BF16), w_ref[...])


def gated_down(hn, xc, up, skip, w, res):
    n, c = hn.shape
    d = w.shape[1]
    tm = _tile(n, 512, SUBLANES)
    return pl.pallas_call(
        _down_body,
        grid=(n // tm,),
        in_specs=[
            pl.BlockSpec((tm, c), lambda i: (i, 0)),
            pl.BlockSpec((tm, c), lambda i: (i, 0)),
            pl.BlockSpec((tm, c), lambda i: (i, 1)),
            pl.BlockSpec((1, c), lambda i: (0, 0)),
            pl.BlockSpec((c, d), lambda i: (0, 0)),
            pl.BlockSpec((tm, d), lambda i: (i, 0)),
        ],
        out_specs=pl.BlockSpec((tm, d), lambda i: (i, 0)),
        out_shape=jax.ShapeDtypeStruct((n, d), F32),
        compiler_params=_cparams("parallel"),
    )(hn, xc, up, skip.reshape(1, c), w, res)


def _ffn_body(x_ref, g_ref, wg_ref, wu_ref, wd_ref, o_ref, xn_ref):
    @pl.when(pl.program_id(1) == 0)
    def _():
        x = x_ref[...]
        xn_ref[...] = _rms(x, g_ref[...]).astype(BF16)
        o_ref[...] = x

    xn = xn_ref[...]
    hmid = _silu(_dot(xn, wg_ref[...])) * _dot(xn, wu_ref[...])
    o_ref[...] += _dot(hmid.astype(BF16), wd_ref[...])


def ffn_dense(x, g, wg, wu, wd):
    n, d = x.shape
    f = wg.shape[1]
    tm = _tile(n, 512, SUBLANES)
    tf = _tile(f, 1408, LANES)
    return pl.pallas_call(
        _ffn_body,
        grid=(n // tm, f // tf),
        in_specs=[
            pl.BlockSpec((tm, d), lambda i, j: (i, 0)),
            pl.BlockSpec((1, d), lambda i, j: (0, 0)),
            pl.BlockSpec((d, tf), lambda i, j: (0, j)),
            pl.BlockSpec((d, tf), lambda i, j: (0, j)),
            pl.BlockSpec((tf, d), lambda i, j: (j, 0)),
        ],
        out_specs=pl.BlockSpec((tm, d), lambda i, j: (i, 0)),
        out_shape=jax.ShapeDtypeStruct((n, d), F32),
        scratch_shapes=[pltpu.VMEM((tm, d), BF16)],
        compiler_params=_cparams("parallel", "arbitrary"),
    )(x, g.reshape(1, d), wg, wu, wd)


def _kv_body(x_ref, g_ref, wk_ref, wv_ref, k32_ref, v32_ref, kb_ref, vb_ref, *, feature_major):
    xn = _rms(x_ref[...], g_ref[...]).astype(BF16)
    k = _dot_nt(wk_ref[...], xn) if feature_major else _dot(xn, wk_ref[...])
    v = _dot(xn, wv_ref[...])
    k32_ref[...] = k
    v32_ref[...] = v
    kb_ref[...] = k.astype(BF16)
    vb_ref[...] = v.astype(BF16)


def shared_kv(x, g, wk, wv, seq_len=None):
    n, d = x.shape
    kw = wk.shape[0] if seq_len else wk.shape[1]
    vw = wv.shape[1]
    tm = _tile(seq_len or n, 512, LANES if seq_len else SUBLANES)
    row_spec = lambda width: pl.BlockSpec((tm, width), lambda i: (i, 0))
    if seq_len:
        steps = seq_len // tm
        k_spec = pl.BlockSpec((None, kw, tm), lambda i: (i // steps, 0, i % steps))
        k_shape = (n // seq_len, kw, seq_len)
    else:
        k_spec, k_shape = row_spec(kw), (n, kw)
    return pl.pallas_call(
        functools.partial(_kv_body, feature_major=bool(seq_len)),
        grid=(n // tm,),
        in_specs=[
            row_spec(d),
            pl.BlockSpec((1, d), lambda i: (0, 0)),
            pl.BlockSpec(wk.shape, lambda i: (0, 0)),
            pl.BlockSpec(wv.shape, lambda i: (0, 0)),
        ],
        out_specs=[k_spec, row_spec(vw), k_spec, row_spec(vw)],
        out_shape=[
            jax.ShapeDtypeStruct(k_shape, F32),
            jax.ShapeDtypeStruct((n, vw), F32),
            jax.ShapeDtypeStruct(k_shape, BF16),
            jax.ShapeDtypeStruct((n, vw), BF16),
        ],
        compiler_params=_cparams("parallel"),
    )(x, g.reshape(1, d), wk, wv)


def _lambda_full(lq1_ref, lk1_ref, lq2_ref, lk2_ref, lam_init):
    a = jnp.sum(lq1_ref[...] * lk1_ref[...], axis=1, keepdims=True)
    b = jnp.sum(lq2_ref[...] * lk2_ref[...], axis=1, keepdims=True)
    return jnp.exp(a) - jnp.exp(b) + lam_init


def _attn_prefill_body(slopes_ref, q_ref, k_ref, v_ref, lq1_ref, lk1_ref, lq2_ref, lk2_ref, gs_ref,
                       o_ref, t_ref, m_ref, al_ref, l_ref, acc_ref, *, tq, dk, lam_init):
    h = pl.program_id(1)
    i = pl.program_id(2)
    nt = tq // LANES
    slope = slopes_ref[h] * LOG2E
    q = q_ref[...]
    lane = lax.broadcasted_iota(jnp.int32, q.shape, 1)
    zero = jnp.zeros_like(q)
    qz = (jnp.where(lane < dk, q, zero), jnp.where(lane >= dk, q, zero))
    r = lax.broadcasted_iota(jnp.int32, (tq, tq), 0)
    c = lax.broadcasted_iota(jnp.int32, (tq, tq), 1)
    rel = r - c
    key_bias = slope * lax.broadcasted_iota(jnp.int32, (1, tq), 1).astype(F32)
    m_ref[1] = jnp.full(m_ref.shape[1:], -jnp.inf, F32)
    l_ref[...] = jnp.zeros(l_ref.shape, F32)
    acc_ref[...] = jnp.zeros(acc_ref.shape, F32)

    def block_bias(j):
        return (-slope) * ((i - j) * tq).astype(F32)

    def scores(j, masked, src, dst):
        kb = k_ref[:, pl.ds(pl.multiple_of(j * tq, tq), tq)]
        cj = block_bias(j)
        for mp in range(2):
            t = _dot(qz[mp], kb) + key_bias
            if masked:
                t = jnp.where(rel >= 0, t, -jnp.inf)
            t_ref[dst, mp] = t
            tm = t[:, 0:LANES]
            for ct in range(1, nt):
                tm = jnp.maximum(tm, t[:, ct * LANES:(ct + 1) * LANES])
            m_old = m_ref[src, mp]
            m_new = jnp.maximum(m_old, jnp.max(tm, axis=1, keepdims=True) + cj)
            al_ref[dst, mp] = jnp.exp2(m_old - m_new)
            m_ref[dst, mp] = m_new

    def accumulate(j, slot):
        vb = v_ref[pl.ds(pl.multiple_of(j * tq, tq), tq), :]
        cj = block_bias(j)
        for mp in range(2):
            mb = m_ref[slot, mp] - cj
            alpha = al_ref[slot, mp]
            ps = []
            lsum = None
            for ct in range(nt):
                pc = jnp.exp2((t_ref[slot, mp, :, ct * LANES:(ct + 1) * LANES] - mb).astype(BF16))
                lsum = pc if lsum is None else lsum + pc
                ps.append(pc)
            l_ref[mp] = alpha * l_ref[mp] + lsum.astype(F32)
            a_acc = alpha if acc_ref.shape[2] == LANES else alpha[:, 0:1]
            acc_ref[mp] = a_acc * acc_ref[mp] + _dot(jnp.concatenate(ps, axis=1), vb)

    @pl.when(i == 0)
    def _():
        scores(0, True, 1, 0)
        accumulate(0, 0)

    @pl.when(i > 0)
    def _():
        scores(0, False, 1, 0)

    n_plain = jnp.maximum(i - 1, 0)

    def pair(jj, carry):
        j = 2 * jj
        accumulate(j, 0)
        scores(j + 1, False, 0, 1)
        accumulate(j + 1, 1)
        scores(j + 2, False, 1, 0)
        return carry

    lax.fori_loop(0, n_plain // 2, pair, 0)

    @pl.when(n_plain % 2 == 1)
    def _():
        accumulate(n_plain - 1, 0)
        scores(n_plain, False, 0, 1)

    @pl.when((i > 0) & (i % 2 == 1))
    def _():
        accumulate(i - 1, 0)
        scores(i, True, 0, 1)
        accumulate(i, 1)

    @pl.when((i > 0) & (i % 2 == 0))
    def _():
        accumulate(i - 1, 1)
        scores(i, True, 1, 0)
        accumulate(i, 0)

    lam = _lambda_full(lq1_ref, lk1_ref, lq2_ref, lk2_ref, lam_init)
    l0 = jnp.sum(l_ref[0], axis=1, keepdims=True)
    l1 = jnp.sum(l_ref[1], axis=1, keepdims=True)
    o = acc_ref[0] / l0 - lam * (acc_ref[1] / l1)
    o_ref[...] = (_rms(o, gs_ref[...]) * (1.0 - lam_init)).astype(BF16)


def attn_prefill(q, k, v, slopes, lq1, lk1, lq2, lk2, g_sub, nh, lam_init):
    bsz, t, qw = q.shape
    dk2 = qw // nh
    dv = v.shape[2] // nh
    tq = _tile(t, 512, LANES)
    body = functools.partial(_attn_prefill_body, tq=tq, dk=dk2 // 2, lam_init=lam_init)
    vec = lambda a: a.reshape(1, -1)
    small = lambda n: pl.BlockSpec((1, n), lambda b, h, i: (0, 0))
    return pl.pallas_call(
        body,
        grid=(bsz, nh, t // tq),
        in_specs=[
            pl.BlockSpec(memory_space=pltpu.SMEM),
            pl.BlockSpec((None, tq, dk2), lambda b, h, i: (b, i, h)),
            pl.BlockSpec((None, dk2, t), lambda b, h, i: (b, h, 0)),
            pl.BlockSpec((None, t, dv), lambda b, h, i: (b, 0, h)),
            small(dk2 // 2), small(dk2 // 2), small(dk2 // 2), small(dk2 // 2), small(dv),
        ],
        out_specs=pl.BlockSpec((None, tq, dv), lambda b, h, i: (b, i, h)),
        out_shape=jax.ShapeDtypeStruct((bsz, t, nh * dv), BF16),
        scratch_shapes=[pltpu.VMEM((2, 2, tq, tq), F32), pltpu.VMEM((2, 2, tq, LANES), F32),
                        pltpu.VMEM((2, 2, tq, LANES), F32), pltpu.VMEM((2, tq, LANES), F32),
                        pltpu.VMEM((2, tq, dv), F32)],
        compiler_params=_cparams("parallel", "parallel", "arbitrary"),
    )(slopes, q, k, v, vec(lq1), vec(lk1), vec(lq2), vec(lk2), vec(g_sub))


def _attn_paged_body(pt_ref, q_ref, *refs, pp, nh, dk, dv, t, page, past, lam_init):
    kp_refs = refs[:pp]
    vp_refs = refs[pp:2 * pp]
    (kn_ref, vn_ref, lq1_ref, lk1_ref, lq2_ref, lk2_ref, gs_ref,
     o_ref, qbd_ref, m_ref, l_ref, acc_ref) = refs[2 * pp:]
    j = pl.program_id(1)
    nj = pl.num_programs(1)
    rows = nh * 2 * t
    hr = 2 * t
    qw = nh * 2 * dk

    @pl.when(j == 0)
    def _():
        qf = q_ref[...].astype(F32)
        qrep = jnp.concatenate([qf] * (2 * nh), axis=0)
        rr = lax.broadcasted_iota(jnp.int32, (rows, qw), 0)
        cc = lax.broadcasted_iota(jnp.int32, (rows, qw), 1)
        qbd_ref[...] = jnp.where(rr // t == cc // dk, qrep, 0.0).astype(BF16)
        m_ref[...] = jnp.full(m_ref.shape, -jnp.inf, F32)
        l_ref[...] = jnp.zeros(l_ref.shape, F32)
        acc_ref[...] = jnp.zeros(acc_ref.shape, F32)

    ri = lax.broadcasted_iota(jnp.int32, (rows, 1), 0)
    head = (ri // hr).astype(F32)
    slope = jnp.exp2(-8.0 * (head + 1.0) / nh) * LOG2E
    qpos = past + ri % t
    kl = lax.broadcasted_iota(jnp.int32, (rows, page), 1)

    def update(blocks):
        ss = []
        for s, _, kpos, valid in blocks:
            dist = qpos - kpos
            s = s - slope * dist.astype(F32)
            if valid is not None:
                s = jnp.where(valid & (dist >= 0), s, -jnp.inf)
            ss.append(s)
        s = jnp.concatenate(ss, axis=1)
        m = m_ref[...]
        m_new = jnp.maximum(m, jnp.max(s, axis=1, keepdims=True))
        p = jnp.exp2(s - m_new)
        alpha = jnp.exp2(m - m_new)
        l_ref[...] = alpha * l_ref[...] + jnp.sum(p, axis=1, keepdims=True)
        m_ref[...] = m_new
        pb = p.astype(BF16)
        for hh in range(nh):
            rs = slice(hh * hr, (hh + 1) * hr)
            vh = jnp.concatenate([blk[1](hh) for blk in blocks], axis=0)
            acc_ref[rs, :] = alpha[rs] * acc_ref[rs, :] + _dot(pb[rs], vh)

    def past_blocks():
        out = []
        for r in range(pp):
            s = _dot(qbd_ref[...], kp_refs[r][...].astype(BF16))
            v_head = lambda hh, r=r: vp_refs[r][pl.ds(hh, page, stride=nh), :].astype(BF16)
            out.append((s, v_head, (j * pp + r) * page + kl, None))
        return out

    @pl.when(j < nj - 1)
    def _():
        update(past_blocks())

    @pl.when(j == nj - 1)
    def _():
        new = (_dot_nt(qbd_ref[...], kn_ref[...]), lambda hh: vn_ref[:, hh * dv:(hh + 1) * dv], past + kl, kl < t)
        update(past_blocks() + [new])
        o = acc_ref[...] / l_ref[...]
        lam = _lambda_full(lq1_ref, lk1_ref, lq2_ref, lk2_ref, lam_init)
        for hh in range(nh):
            r0 = hh * hr
            od = o[r0:r0 + t] - lam * o[r0 + t:r0 + hr]
            o_ref[:, hh * dv:(hh + 1) * dv] = (_rms(od, gs_ref[...]) * (1.0 - lam_init)).astype(BF16)


def attn_paged(q, cache_kt, cache_v, page_table, k_new, v_new, lq1, lk1, lq2, lk2, g_sub, lam_init):
    bsz, t, qw = q.shape
    _, page, nh, dv = cache_v.shape
    vw = nh * dv
    dk = qw // (2 * nh)
    n_pages = page_table.shape[1]
    pp = 4 if n_pages % 4 == 0 else (2 if n_pages % 2 == 0 else 1)
    rows = nh * 2 * t
    kn = jnp.pad(k_new, ((0, 0), (0, page - t), (0, 0)))
    vn = jnp.pad(v_new, ((0, 0), (0, page - t), (0, 0)))
    body = functools.partial(_attn_paged_body, pp=pp, nh=nh, dk=dk, dv=dv, t=t, page=page,
                             past=n_pages * page, lam_init=lam_init)
    vec = lambda a: a.reshape(1, -1)
    small = lambda n: pl.BlockSpec((1, n), lambda b, j, pt: (0, 0))
    k_spec = lambda r: pl.BlockSpec((None, qw, page), lambda b, j, pt: (pt[b, j * pp + r], 0, 0))
    v_spec = lambda r: pl.BlockSpec((None, page * nh, dv), lambda b, j, pt: (pt[b, j * pp + r], 0, 0))
    grid_spec = pltpu.PrefetchScalarGridSpec(
        num_scalar_prefetch=1,
        grid=(bsz, n_pages // pp),
        in_specs=(
            [pl.BlockSpec((None, t, qw), lambda b, j, pt: (b, 0, 0))]
            + [k_spec(r) for r in range(pp)]
            + [v_spec(r) for r in range(pp)]
            + [pl.BlockSpec((None, page, qw), lambda b, j, pt: (b, 0, 0)),
               pl.BlockSpec((None, page, vw), lambda b, j, pt: (b, 0, 0)),
               small(dk), small(dk), small(dk), small(dk), small(dv)]
        ),
        out_specs=pl.BlockSpec((None, t, vw), lambda b, j, pt: (b, 0, 0)),
        scratch_shapes=[
            pltpu.VMEM((rows, qw), BF16),
            pltpu.VMEM((rows, 1), F32),
            pltpu.VMEM((rows, 1), F32),
            pltpu.VMEM((rows, dv), F32),
        ],
    )
    return pl.pallas_call(
        body,
        grid_spec=grid_spec,
        out_shape=jax.ShapeDtypeStruct((bsz, t, vw), BF16),
        compiler_params=_cparams("parallel", "arbitrary"),
    )(page_table, q, *([cache_kt] * pp), *([cache_v.reshape(-1, page * nh, dv)] * pp), kn, vn,
      vec(lq1), vec(lk1), vec(lq2), vec(lk2), vec(g_sub))


def _to_token_tiles(dst_ref, x, n_rows):
    nch = x.shape[1] // LANES
    for c in range(nch):
        dst_ref[pl.ds(c, n_rows, stride=nch), :] = x[:, c * LANES:(c + 1) * LANES]


def _router_body(x_ref, g_ref, wr_ref, xn_ref, idx_ref, gate_ref, *, ne):
    xn = _rms(x_ref[...], g_ref[...])
    xn_ref[...] = xn
    logits = jnp.dot(xn, wr_ref[...], preferred_element_type=F32, precision=lax.Precision.HIGHEST)
    lane = lax.broadcasted_iota(jnp.int32, logits.shape, 1)
    logits = jnp.where(lane < ne, logits, -jnp.inf)
    m1 = jnp.max(logits, axis=1, keepdims=True)
    i1 = jnp.min(jnp.where(logits == m1, lane, LANES), axis=1, keepdims=True)
    rest = jnp.where(lane == i1, -jnp.inf, logits)
    m2 = jnp.max(rest, axis=1, keepdims=True)
    i2 = jnp.min(jnp.where(rest == m2, lane, LANES), axis=1, keepdims=True)
    e = jnp.exp(m2 - m1)
    g1 = 1.0 / (1.0 + e)
    g2 = e / (1.0 + e)
    idx_ref[...] = jnp.where(lane == 0, i1, jnp.where(lane == 1, i2, 0))
    gate_ref[...] = jnp.where(lane == 0, g1, jnp.where(lane == 1, g2, 0.0))


def moe_router(x, g, w_router):
    n, d = x.shape
    ne = w_router.shape[1]
    wr = jnp.zeros((d, LANES), F32).at[:, :ne].set(w_router)
    tm = _tile(n, 512, SUBLANES)
    return pl.pallas_call(
        functools.partial(_router_body, ne=ne),
        grid=(n // tm,),
        in_specs=[
            pl.BlockSpec((tm, d), lambda i: (i, 0)),
            pl.BlockSpec((1, d), lambda i: (0, 0)),
            pl.BlockSpec((d, LANES), lambda i: (0, 0)),
        ],
        out_specs=[
            pl.BlockSpec((tm, d), lambda i: (i, 0)),
            pl.BlockSpec((tm, LANES), lambda i: (i, 0)),
            pl.BlockSpec((tm, LANES), lambda i: (i, 0)),
        ],
        out_shape=[
            jax.ShapeDtypeStruct((n, d), F32),
            jax.ShapeDtypeStruct((n, LANES), jnp.int32),
            jax.ShapeDtypeStruct((n, LANES), F32),
        ],
        compiler_params=_cparams("parallel"),
    )(x, g.reshape(1, d), wr)


DMA_ISSUE_UNROLL = 8


def _dispatch_body(pos_ref, x_ref, xs_init_hbm, xs_hbm, sem, *, tc):
    del xs_init_hbm
    base = pl.program_id(0) * tc

    def issue(r, carry):
        for kk in range(TOP_K):
            p = pos_ref[TOP_K * (base + r) + kk]
            pltpu.make_async_copy(x_ref.at[pl.ds(r, 1)], xs_hbm.at[pl.ds(p, 1)], sem).start()
        return carry

    lax.fori_loop(0, tc, issue, 0, unroll=DMA_ISSUE_UNROLL)
    for kk in range(TOP_K):
        pltpu.make_async_copy(x_ref, xs_hbm.at[pl.ds(0, tc)], sem).wait()


def dispatch_rows(x, pos, n_rows):
    n, d = x.shape
    tc = _tile(n, 1024, SUBLANES)
    grid_spec = pltpu.PrefetchScalarGridSpec(
        num_scalar_prefetch=1,
        grid=(n // tc,),
        in_specs=[pl.BlockSpec((tc, d), lambda i, pos: (i, 0)), pl.BlockSpec(memory_space=pl.ANY)],
        out_specs=pl.BlockSpec(memory_space=pl.ANY),
        scratch_shapes=[pltpu.SemaphoreType.DMA(())],
    )
    return pl.pallas_call(
        functools.partial(_dispatch_body, tc=tc),
        grid_spec=grid_spec,
        out_shape=jax.ShapeDtypeStruct((n_rows, d), x.dtype),
        input_output_aliases={2: 0},
        compiler_params=pltpu.CompilerParams(dimension_semantics=("arbitrary",),
                                             vmem_limit_bytes=VMEM_LIMIT_BYTES),
    )(pos, x, jnp.zeros((n_rows, d), x.dtype))


def _moe_ffn_body(te_ref, tv_ref, xs_ref, wg_ref, wu_ref, wd_ref, o_ref, xb_ref, acc_ref):
    i = pl.program_id(0)
    j = pl.program_id(1)
    tm, d = acc_ref.shape
    nch = d // LANES

    @pl.when(tv_ref[i] == 0)
    def _():
        @pl.when(j == 0)
        def _():
            o_ref[...] = jnp.zeros(o_ref.shape, F32)

    @pl.when(tv_ref[i] > 0)
    def _():
        @pl.when(j == 0)
        def _():
            xb_ref[...] = xs_ref[...].astype(BF16)

        xb = xb_ref[...]
        hmid = _silu(_dot(xb, wg_ref[...])) * _dot(xb, wu_ref[...])
        y = _dot(hmid.astype(BF16), wd_ref[...])

        @pl.when(j == 0)
        def _():
            acc_ref[...] = y

        @pl.when(j > 0)
        def _():
            acc_ref[...] += y

        @pl.when(j == pl.num_programs(1) - 1)
        def _():
            _to_token_tiles(o_ref, acc_ref[...], tm)


def moe_ffn(xs, tile_expert, tile_valid, wg, wu, wd, tm):
    r, d = xs.shape
    nch = d // LANES
    f = wg.shape[2]
    tf = _tile(f, 1408, LANES)
    grid_spec = pltpu.PrefetchScalarGridSpec(
        num_scalar_prefetch=2,
        grid=(r // tm, f // tf),
        in_specs=[
            pl.BlockSpec((tm, d), lambda i, j, te, tv: (i, 0)),
            pl.BlockSpec((None, d, tf), lambda i, j, te, tv: (te[i], 0, j)),
            pl.BlockSpec((None, d, tf), lambda i, j, te, tv: (te[i], 0, j)),
            pl.BlockSpec((None, tf, d), lambda i, j, te, tv: (te[i], j, 0)),
        ],
        out_specs=pl.BlockSpec((tm * nch, LANES), lambda i, j, te, tv: (i, 0)),
        scratch_shapes=[pltpu.VMEM((tm, d), BF16), pltpu.VMEM((tm, d), F32)],
    )
    return pl.pallas_call(
        _moe_ffn_body,
        grid_spec=grid_spec,
        out_shape=jax.ShapeDtypeStruct((r * nch, LANES), F32),
        compiler_params=_cparams("parallel", "arbitrary"),
    )(tile_expert, tile_valid, xs, wg, wu, wd)


def _combine_body(pos_ref, x_ref, gate_ref, gf_ref, ys_hbm, o_ref, buf_ref, sem, *, tc, nch):
    i = pl.program_id(0)
    n_steps = pl.num_programs(0)

    def fetch(step, slot):
        base = step * tc

        def issue(r, carry):
            for kk in range(TOP_K):
                p = pos_ref[TOP_K * (base + r) + kk]
                pltpu.make_async_copy(ys_hbm.at[pl.ds(pl.multiple_of(p * nch, nch), nch)],
                                      buf_ref.at[slot, kk, pl.ds(pl.multiple_of(r * nch, nch), nch)],
                                      sem.at[slot]).start()
            return carry

        lax.fori_loop(0, tc, issue, 0, unroll=DMA_ISSUE_UNROLL)

    slot = i % 2

    @pl.when(i == 0)
    def _():
        fetch(0, 0)

    @pl.when(i + 1 < n_steps)
    def _():
        fetch(i + 1, 1 - slot)

    for kk in range(TOP_K):
        pltpu.make_async_copy(ys_hbm.at[pl.ds(0, tc * nch)], buf_ref.at[slot, kk], sem.at[slot]).wait()
    gate = gate_ref[...]
    ys = []
    ssq = jnp.zeros((tc, 1), F32)
    for c in range(nch):
        y = x_ref[:, c * LANES:(c + 1) * LANES]
        for kk in range(TOP_K):
            y = y + gate[:, kk:kk + 1] * buf_ref[slot, kk, pl.ds(c, tc, stride=nch), :]
        ssq = ssq + jnp.sum(y * y, axis=1, keepdims=True)
        ys.append(y)
    scale = lax.rsqrt(ssq / (nch * LANES) + EPS)
    for c in range(nch):
        o_ref[:, c * LANES:(c + 1) * LANES] = ys[c] * scale * gf_ref[:, c * LANES:(c + 1) * LANES]


def moe_combine_norm(x, gates, pos, ys, g_final):
    n, d = x.shape
    nch = d // LANES
    tc = _tile(n, 256, SUBLANES)
    grid_spec = pltpu.PrefetchScalarGridSpec(
        num_scalar_prefetch=1,
        grid=(n // tc,),
        in_specs=[
            pl.BlockSpec((tc, d), lambda i, pos: (i, 0)),
            pl.BlockSpec((tc, LANES), lambda i, pos: (i, 0)),
            pl.BlockSpec((1, d), lambda i, pos: (0, 0)),
            pl.BlockSpec(memory_space=pl.ANY),
        ],
        out_specs=pl.BlockSpec((tc, d), lambda i, pos: (i, 0)),
        scratch_shapes=[pltpu.VMEM((2, TOP_K, tc * nch, LANES), F32), pltpu.SemaphoreType.DMA((2,))],
    )
    return pl.pallas_call(
        functools.partial(_combine_body, tc=tc, nch=nch),
        grid_spec=grid_spec,
        out_shape=jax.ShapeDtypeStruct((n, d), F32),
        compiler_params=pltpu.CompilerParams(dimension_semantics=("arbitrary",),
                                             vmem_limit_bytes=VMEM_LIMIT_BYTES),
    )(pos, x, gates, g_final.reshape(1, d), ys)


def _routing_tables(top_i, ne, tm):
    n = top_i.shape[0]
    e_flat = top_i.reshape(-1)
    onehot = (e_flat[:, None] == jnp.arange(ne, dtype=jnp.int32)[None, :]).astype(jnp.int32)
    csum = jnp.cumsum(onehot, axis=0)
    rank = jnp.sum(onehot * csum, axis=1) - 1
    counts = csum[-1]
    padded = ((counts + tm - 1) // tm) * tm
    pad_end = jnp.cumsum(padded)
    pad_start = pad_end - padded
    pos = jnp.sum(onehot * pad_start[None, :], axis=1) + rank
    n_tiles = (n * TOP_K) // tm + ne
    tile_row0 = jnp.arange(n_tiles, dtype=jnp.int32) * tm
    tile_valid = (tile_row0 < pad_end[-1]).astype(jnp.int32)
    last_e = jnp.max(jnp.where(counts > 0, jnp.arange(ne, dtype=jnp.int32), 0))
    tile_expert = jnp.sum((pad_end[None, :] <= tile_row0[:, None]).astype(jnp.int32), axis=1)
    tile_expert = jnp.minimum(tile_expert, last_e)
    return pos.astype(jnp.int32), tile_expert.astype(jnp.int32), tile_valid


def moe_block_final(x, g_ffn, w_router, wg, wu, wd, g_final):
    n, d = x.shape
    ne = w_router.shape[1]
    tm = _tile(n * TOP_K, 512, SUBLANES)
    xn, idx, gates = moe_router(x, g_ffn, w_router)
    pos, tile_expert, tile_valid = _routing_tables(idx[:, :TOP_K], ne, tm)
    xs = dispatch_rows(xn, pos, tile_expert.shape[0] * tm)
    ys = moe_ffn(xs, tile_expert, tile_valid, wg, wu, wd, tm)
    return moe_combine_norm(x, gates, pos, ys, g_final)


def _run_group(x, c0, n0, m0, conv0, past, w):
    bsz, t, d = x.shape
    n = bsz * t
    nh_ml = c0.shape[1]
    xf = x.reshape(n, d)

    up = rms_matmul(xf, w["ml_norm"], w["ml_w_up"], BF16)
    inner = up.shape[1] // 2
    xc, conv_new = conv_silu(up.reshape(bsz, t, 2 * inner), conv0, w["ml_conv_w"], w["ml_conv_b"])
    xc = xc.reshape(n, inner)
    qk = matmul(xc, w["ml_w_qk"], BF16)
    v = matmul(up, w["ml_w_v"], BF16, col_block=0)
    gates = mlstm_gates(qk, v, w["ml_w_ig"], w["ml_b_ig"], w["ml_w_fg"], w["ml_b_fg"])
    hn, c_new, n_new, m_new = mlstm_scan(qk.reshape(bsz, t, 2 * inner), v.reshape(bsz, t, inner),
                                         gates.reshape(bsz, t, LANES), c0, n0, m0, w["ml_head_norm"])
    x1 = gated_down(hn.reshape(n, inner), xc, up, w["ml_skip"], w["ml_w_down"], xf)
    x2 = ffn_dense(x1, w["ffn_norm0"], w["mlp_w_gate"], w["mlp_w_up"], w["mlp_w_down"])

    nh_da = w["da_heads"]
    kw = w["da_w_q"].shape[1]
    dk = kw // (2 * nh_da)
    long_seq = past is None and t % LANES == 0
    if long_seq:
        k32, v32, kb, vb = shared_kv(x2, w["kv_norm"], w["w_k_t"], w["w_v"], seq_len=t)
        k_leaf = jnp.transpose(k32.reshape(bsz, nh_da, 2, dk, t), (0, 4, 1, 2, 3))
    else:
        k32, v32, kb, vb = shared_kv(x2, w["kv_norm"], w["w_k"], w["w_v"])
        k_leaf = k32.reshape(bsz, t, nh_da, 2, dk)
    vw = v32.shape[1]

    q = rms_matmul(x2, w["da_norm"], w["da_w_q"], BF16).reshape(bsz, t, kw)
    lam_init = w["lam_init"]
    lam_args = (w["da_lq1"], w["da_lk1"], w["da_lq2"], w["da_lk2"], w["da_subln"])
    if past is None:
        kt = kb if long_seq else jnp.transpose(kb.reshape(bsz, t, kw), (0, 2, 1))
        o = attn_prefill(q, kt, vb.reshape(bsz, t, vw), w["slopes"], *lam_args, nh_da, lam_init)
    else:
        cache_kt, cache_v, page_table = past
        o = attn_paged(q, cache_kt, cache_v, page_table, kb.reshape(bsz, t, kw), vb.reshape(bsz, t, vw),
                       *lam_args, lam_init)
    x3 = matmul_res(o.reshape(n, vw), w["da_w_o"], x2)
    y = moe_block_final(x3, w["ffn_norm1"], w["moe_router"], w["moe_w_gate"], w["moe_w_up"], w["moe_w_down"],
                        w["final_norm"])
    return (y.reshape(bsz, t, d), c_new[None], n_new[None], m_new[None], conv_new[None],
            k_leaf, v32.reshape(bsz, t, nh_da, vw // nh_da))


def kernel(x_prompt, x_sample, state_mlstm_C, state_mlstm_n, state_mlstm_m, state_conv, cache_k, cache_v, page_table, ml_norm, ml_w_up, ml_conv_w, ml_conv_b, ml_w_q, ml_w_k, ml_w_v, ml_w_ig, ml_b_ig, ml_w_fg, ml_b_fg, ml_head_norm, ml_skip, ml_w_down, kv_norm, w_kv, da_norm, da_w_q, da_lq1, da_lk1, da_lq2, da_lk2, da_subln, da_w_o, ffn_norm, mlp_w_gate, mlp_w_up, mlp_w_down, moe_router, moe_w_gate, moe_w_up, moe_w_down, final_norm):
    assert ml_norm.shape[0] == 1 and da_norm.shape[0] == 1 and ffn_norm.shape[0] == 2, "one mLSTM layer then one attention layer"
    nh_da = cache_k.shape[2]
    dk = cache_k.shape[4]
    layer = 1
    w = dict(
        ml_norm=ml_norm[0], ml_w_up=ml_w_up[0].astype(BF16), ml_conv_w=ml_conv_w[0], ml_conv_b=ml_conv_b[0],
        ml_w_qk=jnp.concatenate([ml_w_q[0], ml_w_k[0]], axis=1).astype(BF16), ml_w_v=ml_w_v[0].astype(BF16),
        ml_w_ig=ml_w_ig[0], ml_b_ig=ml_b_ig[0], ml_w_fg=ml_w_fg[0], ml_b_fg=ml_b_fg[0],
        ml_head_norm=ml_head_norm[0], ml_skip=ml_skip[0], ml_w_down=ml_w_down[0].astype(BF16),
        kv_norm=kv_norm, w_k=w_kv[:, :da_w_q.shape[2]].astype(BF16), w_k_t=w_kv[:, :da_w_q.shape[2]].T.astype(BF16),
        w_v=w_kv[:, da_w_q.shape[2]:].astype(BF16), da_norm=da_norm[0],
        da_w_q=(da_w_q[0] * (float(dk) ** -0.5 * LOG2E)).astype(BF16),
        da_lq1=da_lq1[0], da_lk1=da_lk1[0], da_lq2=da_lq2[0], da_lk2=da_lk2[0], da_subln=da_subln[0],
        da_w_o=da_w_o[0].astype(BF16), ffn_norm0=ffn_norm[0], ffn_norm1=ffn_norm[1],
        mlp_w_gate=mlp_w_gate[0].astype(BF16), mlp_w_up=mlp_w_up[0].astype(BF16), mlp_w_down=mlp_w_down[0].astype(BF16),
        moe_router=moe_router[0], moe_w_gate=moe_w_gate[0].astype(BF16), moe_w_up=moe_w_up[0].astype(BF16),
        moe_w_down=moe_w_down[0].astype(BF16), final_norm=final_norm,
        da_heads=nh_da, lam_init=0.8 - 0.6 * math.exp(-0.3 * layer),
        slopes=jnp.exp2(-8.0 * jnp.arange(1, nh_da + 1, dtype=F32) / nh_da),
    )
    bp = x_prompt.shape[0]
    _, _, nh_ml, hd, _ = state_mlstm_C.shape
    kc = state_conv.shape[2]
    inner = state_conv.shape[3]
    zeros = lambda *s: jnp.zeros(s, F32)
    out_p = _run_group(x_prompt, zeros(bp, nh_ml, hd, hd), zeros(bp, nh_ml, hd), zeros(bp, nh_ml),
                       zeros(bp, kc, inner), None, w)
    pool, page = cache_k.shape[0], cache_k.shape[1]
    cache_kt = jnp.transpose(cache_k, (0, 2, 3, 4, 1)).reshape(pool, -1, page)
    past = (cache_kt, cache_v, page_table)
    out_s = _run_group(x_sample, state_mlstm_C[0], state_mlstm_n[0], state_mlstm_m[0], state_conv[0], past, w)
    y_p, p_c, p_n, p_m, p_conv, p_k, p_v = out_p
    y_s, s_c, s_n, s_m, s_conv, s_k, s_v = out_s
    return (y_p, y_s, p_c, p_n, p_m, p_conv, p_k, p_v, s_c, s_n, s_m, s_conv, s_k, s_v)
```
